```python
import jax, jax.numpy as jnp
from jax import lax
import numpy as np

D_MODEL = 1024
BATCH = 4
SEQ = 8192
DEPTH = 4

CHUNK = 64
HEAD_DIM = 64
N_HEADS_TOTAL = D_MODEL // HEAD_DIM
N_SC_GROUPS = N_HEADS_TOTAL // 4
N_LRU_HEADS = (N_HEADS_TOTAL - N_SC_GROUPS) // 2
N_SB_HEADS = N_HEADS_TOTAL - N_SC_GROUPS - N_LRU_HEADS
D_LRU = N_LRU_HEADS * HEAD_DIM
D_SB = N_SB_HEADS * HEAD_DIM
D_SC = N_SC_GROUPS * HEAD_DIM
D_MIX = D_LRU + D_SB + D_SC
D_IN = 2 * D_LRU + 3 * D_SB + 3 * D_SC
LRU_CONV = 4
SC_CONV = 3
LRU_C = 8.0
Q_BLOCK = 2 * CHUNK
D_FF = 256 * (-(-8 * D_MODEL // (3 * 256)))
N_EXPERTS = 8
TOP_K = 2
N_DENSE = (DEPTH + 1) // 2
N_MOE = DEPTH // 2
EPS = 1e-6

kernel_name = "hybrid_rglru_stickbreak_shortconv_moe_trunk"


def rms_norm(x, g):
    xf = x.astype(jnp.float32)
    y = xf * lax.rsqrt(jnp.mean(xf * xf, axis=-1, keepdims=True) + EPS)
    return (y * g.astype(jnp.float32)).astype(x.dtype)


def causal_depthwise_conv(x, w):
    k_width, ch = w.shape
    return lax.conv_general_dilated(
        x, w[:, None, :].astype(x.dtype), window_strides=(1,),
        padding=[(k_width - 1, 0)], dimension_numbers=("NWC", "WIO", "NWC"),
        feature_group_count=ch)


def rg_lru(x, w_a, b_a, w_x, b_x, lam):
    bn, s, _ = x.shape
    xh = x.reshape(bn, s, N_LRU_HEADS, HEAD_DIM)
    gate_r = jax.nn.sigmoid(jnp.einsum("bshi,hij->bshj", xh, w_a).reshape(bn, s, D_LRU) + b_a)
    gate_i = jax.nn.sigmoid(jnp.einsum("bshi,hij->bshj", xh, w_x).reshape(bn, s, D_LRU) + b_x)
    log_a = (LRU_C * gate_r.astype(jnp.float32)) * jax.nn.log_sigmoid(lam.astype(jnp.float32))
    a = jnp.exp(log_a)
    u = jnp.sqrt(-jnp.expm1(2.0 * log_a)) * (gate_i * x).astype(jnp.float32)

    def combine(c1, c2):
        a1, b1 = c1
        a2, b2 = c2
        return a1 * a2, a2 * b1 + b2

    _, h = lax.associative_scan(combine, (a, u), axis=1)
    return h.astype(x.dtype)


def stick_breaking_attention(q, k, v):
    bn, nh, s, dh = q.shape
    nblk = s // Q_BLOCK
    qf = q.astype(jnp.float32) * (dh ** -0.5)
    kf = k.astype(jnp.float32)
    vf = v.astype(jnp.float32)
    qb = qf.reshape(bn, nh, nblk, Q_BLOCK, dh).transpose(2, 0, 1, 3, 4)
    key_pos = jnp.arange(s)

    def block(args):
        i, qi = args
        q_pos = i * Q_BLOCK + jnp.arange(Q_BLOCK)
        strict = key_pos[None, :] < q_pos[:, None]
        z = jnp.einsum("bhqd,bhkd->bhqk", qi, kf)
        log_beta = jax.nn.log_sigmoid(z)
        log_keep = jnp.where(strict, log_beta - z, 0.0)
        later = lax.cumsum(log_keep, axis=3, reverse=True) - log_keep
        w = jnp.where(strict, jnp.exp(log_beta + later), 0.0)
        return jnp.einsum("bhqk,bhkd->bhqd", w, vf)

    out = lax.map(block, (jnp.arange(nblk), qb))
    return out.transpose(1, 2, 0, 3, 4).reshape(bn, nh, s, dh).astype(v.dtype)


def hybrid_mixer(h, w_in, lru_conv_w, lru_conv_b, lru_wa, lru_ba, lru_wx, lru_bx, lru_lam,
                 sc_conv_w, mix_out_g, w_out):
    bn, s, _ = h.shape
    proj = h @ w_in
    sizes = (D_LRU, D_LRU, D_SB, D_SB, D_SB, D_SC, D_SC)
    cuts = [sum(sizes[:i + 1]) for i in range(len(sizes))]
    lru_x, lru_g, q, k, v, sc_b, sc_c, sc_x = jnp.split(proj, cuts, axis=-1)
    lru_in = causal_depthwise_conv(lru_x, lru_conv_w) + lru_conv_b
    y_lru = rg_lru(lru_in, lru_wa, lru_ba, lru_wx, lru_bx, lru_lam) * jax.nn.gelu(lru_g)
    to_heads = lambda t: t.reshape(bn, s, N_SB_HEADS, HEAD_DIM).transpose(0, 2, 1, 3)
    y_sb = stick_breaking_attention(to_heads(q), to_heads(k), to_heads(v))
    y_sb = y_sb.transpose(0, 2, 1, 3).reshape(bn, s, D_SB)
    y_sc = sc_b * causal_depthwise_conv(sc_c * sc_x, sc_conv_w)
    y = jnp.concatenate([y_lru, y_sb, y_sc], axis=-1).astype(jnp.float32)
    yh = y.reshape(bn, s, D_MIX // HEAD_DIM, HEAD_DIM)
    yh = yh * lax.rsqrt(jnp.mean(yh * yh, axis=-1, keepdims=True) + EPS)
    y = (yh.reshape(bn, s, D_MIX) * mix_out_g.astype(jnp.float32)).astype(h.dtype)
    return y @ w_out


def swiglu(x, wg, wu, wd):
    return (jax.nn.silu(x @ wg) * (x @ wu)) @ wd


def moe_swiglu(x, w_router, wg, wu, wd):
    logits = (x @ w_router).astype(jnp.float32)
    top_val, top_idx = lax.top_k(logits, TOP_K)
    top_w = jax.nn.softmax(top_val, axis=-1)
    gates = jnp.sum(jax.nn.one_hot(top_idx, N_EXPERTS, dtype=jnp.float32) * top_w[..., None], axis=-2)
    out = jnp.zeros(x.shape, jnp.float32)
    for e in range(N_EXPERTS):
        out = out + gates[..., e:e + 1] * swiglu(x, wg[e], wu[e], wd[e]).astype(jnp.float32)
    return out.astype(x.dtype)


def setup_inputs(seed: int = 0) -> dict:
    key = jax.random.key(seed)
    ks = jax.random.split(key, 24)
    f32 = jnp.float32
    nrm = lambda k, shape, fan_in: jax.random.normal(k, shape, f32) * (fan_in ** -0.5)
    small = lambda k, shape: 0.01 * jax.random.normal(k, shape, f32)
    gain = lambda k, shape: 1.0 + 0.02 * jax.random.normal(k, shape, f32)
    u = jax.random.uniform(ks[9], (DEPTH, D_LRU), f32, 0.9, 0.999)
    a0 = u ** (1.0 / LRU_C)
    lru_lam = jnp.log(a0) - jnp.log1p(-a0)
    return {
        "x": jax.random.normal(ks[0], (BATCH, SEQ, D_MODEL), f32),
        "mix_norm_g": gain(ks[1], (DEPTH, D_MODEL)),
        "w_in": nrm(ks[2], (DEPTH, D_MODEL, D_IN), D_MODEL),
        "lru_conv_w": nrm(ks[3], (DEPTH, LRU_CONV, D_LRU), LRU_CONV),
        "lru_conv_b": small(ks[4], (DEPTH, D_LRU)),
        "lru_wa": nrm(ks[5], (DEPTH, N_LRU_HEADS, HEAD_DIM, HEAD_DIM), HEAD_DIM),
        "lru_ba": small(ks[6], (DEPTH, D_LRU)),
        "lru_wx": nrm(ks[7], (DEPTH, N_LRU_HEADS, HEAD_DIM, HEAD_DIM), HEAD_DIM),
        "lru_bx": small(ks[8], (DEPTH, D_LRU)),
        "lru_lam": lru_lam,
        "sc_conv_w": nrm(ks[10], (DEPTH, SC_CONV, D_SC), SC_CONV),
        "mix_out_g": gain(ks[11], (DEPTH, D_MIX)),
        "w_out": nrm(ks[12], (DEPTH, D_MIX, D_MODEL), D_MIX),
        "ffn_norm_g": gain(ks[13], (DEPTH, D_MODEL)),
        "dense_wg": nrm(ks[14], (N_DENSE, D_MODEL, D_FF), D_MODEL),
        "dense_wu": nrm(ks[15], (N_DENSE, D_MODEL, D_FF), D_MODEL),
        "dense_wd": nrm(ks[16], (N_DENSE, D_FF, D_MODEL), D_FF),
        "router_w": nrm(ks[17], (N_MOE, D_MODEL, N_EXPERTS), D_MODEL),
        "moe_wg": nrm(ks[18], (N_MOE, N_EXPERTS, D_MODEL, D_FF), D_MODEL),
        "moe_wu": nrm(ks[19], (N_MOE, N_EXPERTS, D_MODEL, D_FF), D_MODEL),
        "moe_wd": nrm(ks[20], (N_MOE, N_EXPERTS, D_FF, D_MODEL), D_FF),
        "final_norm_g": gain(ks[21], (D_MODEL,)),
    }


def reference(x, mix_norm_g, w_in, lru_conv_w, lru_conv_b, lru_wa, lru_ba, lru_wx, lru_bx, lru_lam,
              sc_conv_w, mix_out_g, w_out, ffn_norm_g, dense_wg, dense_wu, dense_wd,
              router_w, moe_wg, moe_wu, moe_wd, final_norm_g):
    h = x
    for l in range(DEPTH):
        h = h + hybrid_mixer(rms_norm(h, mix_norm_g[l]), w_in[l], lru_conv_w[l], lru_conv_b[l],
                             lru_wa[l], lru_ba[l], lru_wx[l], lru_bx[l], lru_lam[l],
                             sc_conv_w[l], mix_out_g[l], w_out[l])
        hn = rms_norm(h, ffn_norm_g[l])
        j = l // 2
        if l % 2 == 0:
            h = h + swiglu(hn, dense_wg[j], dense_wu[j], dense_wd[j])
        else:
            h = h + moe_swiglu(hn, router_w[j], moe_wg[j], moe_wu[j], moe_wd[j])
    return rms_norm(h, final_norm_g)
```

```python
import functools

import jax
import jax.numpy as jnp
from jax import lax
from jax.experimental import pallas as pl
from jax.experimental.pallas import tpu as pltpu

F32 = jnp.float32
BF16 = jnp.bfloat16

D_MODEL = 1024
HEAD_DIM = 64
D_LRU = 384
D_SB = 384
D_SC = 256
D_MIX = D_LRU + D_SB + D_SC
LRU_CONV = 4
SC_CONV = 3
LRU_C = 8.0
D_FF = 2816
N_EXPERTS = 8
EPS = 1e-6

V7X_VMEM_LIMIT_BYTES = 56 * 1024 * 1024
SUBLANES = 8
LANES = 128

ROW_TILE = 512
LRU_CHUNK = 256
ATT_BLOCK = 256
FF_TILE = 1408


def _params(*sem):
    return pltpu.CompilerParams(dimension_semantics=sem, vmem_limit_bytes=V7X_VMEM_LIMIT_BYTES)


def _rms_norm_rows(x, g):
    ms = jnp.mean(x * x, axis=-1, keepdims=True)
    return x * lax.rsqrt(ms + EPS) * g


def _sigmoid(x):
    return 1.0 / (1.0 + jnp.exp(-x))


def _inproj_kernel(h_ref, g_ref, w_ref, lru_ref, qkv_ref, sc_ref):
    xn = _rms_norm_rows(h_ref[...], g_ref[...]).astype(BF16)
    c0, c1 = 2 * D_LRU, 2 * D_LRU + 3 * D_SB
    lru_ref[...] = jnp.dot(xn, w_ref[:, 0:c0], preferred_element_type=F32).astype(BF16)
    qkv_ref[...] = jnp.dot(xn, w_ref[:, c0:c1], preferred_element_type=F32).astype(BF16)
    sc_ref[...] = jnp.dot(xn, w_ref[:, c1:], preferred_element_type=F32).astype(BF16)


def _inproj(h, g, w):
    t = h.shape[0]
    d_in = w.shape[1]
    row = lambda c: pl.BlockSpec((ROW_TILE, c), lambda i: (i, 0))
    full = lambda a: pl.BlockSpec(a.shape, lambda i: (0, 0))
    return pl.pallas_call(
        _inproj_kernel,
        grid=(t // ROW_TILE,),
        in_specs=[row(D_MODEL), full(g), full(w)],
        out_specs=[row(2 * D_LRU), row(3 * D_SB), row(3 * D_SC)],
        out_shape=[jax.ShapeDtypeStruct((t, 2 * D_LRU), BF16),
                   jax.ShapeDtypeStruct((t, 3 * D_SB), BF16),
                   jax.ShapeDtypeStruct((t, d_in - 2 * D_LRU - 3 * D_SB), BF16)],
        compiler_params=_params("parallel"),
        name="norm_inproj",
    )(h, g, w)


def _lru_sc_kernel(lru_ref, sc_ref, cw_ref, cb_ref, wgate_ref, bgate_ref, lam_ref, scw_ref,
                   ylru_ref, ysc_ref, xbuf, pbuf, abuf, bbuf, hcar):
    tc = lru_ref.shape[0]
    pad = tc // 2

    @pl.when(pl.program_id(1) == 0)
    def _():
        xbuf[0:SUBLANES, :] = jnp.zeros((SUBLANES, D_LRU), F32)
        pbuf[0:SUBLANES, :] = jnp.zeros((SUBLANES, D_SC), F32)
        hcar[...] = jnp.zeros_like(hcar)
        abuf[0:pad, :] = jnp.ones((pad, D_LRU), F32)
        bbuf[0:pad, :] = jnp.zeros((pad, D_LRU), F32)

    xbuf[SUBLANES:SUBLANES + tc, :] = lru_ref[:, 0:D_LRU].astype(F32)
    conv = cb_ref[...]
    for k in range(LRU_CONV):
        off = SUBLANES - (LRU_CONV - 1) + k
        conv = conv + cw_ref[k:k + 1, :] * xbuf[pl.ds(off, tc), :]
    xbuf[0:SUBLANES, :] = xbuf[tc:tc + SUBLANES, :]

    gates = jnp.dot(conv.astype(BF16), wgate_ref[...], preferred_element_type=F32) + bgate_ref[...]
    gate_r = _sigmoid(gates[:, 0:D_LRU])
    gate_i = _sigmoid(gates[:, D_LRU:])
    lam = lam_ref[...]
    log_sig_lam = jnp.minimum(lam, 0.0) - jnp.log1p(jnp.exp(-jnp.abs(lam)))
    log_a = (LRU_C * gate_r) * log_sig_lam
    a = jnp.exp(log_a)
    u = jnp.sqrt(1.0 - a * a) * (gate_i * conv)

    s = 1
    while s < tc:
        abuf[pad:pad + tc, :] = a
        bbuf[pad:pad + tc, :] = u
        a_sh = abuf[pl.ds(pad - s, tc), :]
        u_sh = bbuf[pl.ds(pad - s, tc), :]
        u = a * u_sh + u
        a = a * a_sh
        s *= 2
    h = a * hcar[0:1, :] + u
    hcar[...] = jnp.broadcast_to(h[tc - 1:tc, :], hcar.shape)

    ylru_ref[...] = (h * jax.nn.gelu(lru_ref[:, D_LRU:].astype(F32))).astype(BF16)

    sc_b = sc_ref[:, 0:D_SC].astype(F32)
    pbuf[SUBLANES:SUBLANES + tc, :] = (sc_ref[:, D_SC:2 * D_SC].astype(F32)
                                       * sc_ref[:, 2 * D_SC:].astype(F32))
    acc = jnp.zeros((tc, D_SC), F32)
    for k in range(SC_CONV):
        off = SUBLANES - (SC_CONV - 1) + k
        acc = acc + scw_ref[k:k + 1, :] * pbuf[pl.ds(off, tc), :]
    pbuf[0:SUBLANES, :] = pbuf[tc:tc + SUBLANES, :]
    ysc_ref[...] = (sc_b * acc).astype(BF16)


def _lru_sc(lru, sc, cw, cb, wgate, bgate, lam, scw, batch, seq):
    tc = LRU_CHUNK
    nt = seq // tc
    row = lambda c: pl.BlockSpec((tc, c), lambda b, t: (b * nt + t, 0))
    full = lambda a: pl.BlockSpec(a.shape, lambda b, t: (0, 0))
    return pl.pallas_call(
        _lru_sc_kernel,
        grid=(batch, nt),
        in_specs=[row(2 * D_LRU), row(3 * D_SC), full(cw), full(cb), full(wgate), full(bgate),
                  full(lam), full(scw)],
        out_specs=[row(D_LRU), row(D_SC)],
        out_shape=[jax.ShapeDtypeStruct((batch * seq, D_LRU), BF16),
                   jax.ShapeDtypeStruct((batch * seq, D_SC), BF16)],
        scratch_shapes=[pltpu.VMEM((tc + SUBLANES, D_LRU), F32),
                        pltpu.VMEM((tc + SUBLANES, D_SC), F32),
                        pltpu.VMEM((tc + tc // 2, D_LRU), F32),
                        pltpu.VMEM((tc + tc // 2, D_LRU), F32),
                        pltpu.VMEM((SUBLANES, D_LRU), F32)],
        compiler_params=_params("parallel", "arbitrary"),
        name="lru_shortconv",
    )(lru, sc, cw, cb, wgate, bgate, lam, scw)


def _sb_attn_kernel(q_ref, k_ref, v_ref, cm_ref, o_ref, o_acc, c_acc):
    blk = q_ref.shape[0]
    i = pl.program_id(2)
    head0 = lax.broadcasted_iota(jnp.int32, (1, LANES), 1) < HEAD_DIM

    q = q_ref[...] * jnp.asarray(HEAD_DIM ** -0.5, BF16)
    zero = jnp.zeros_like(q)
    qq = jnp.concatenate([jnp.where(head0, q, zero), jnp.where(head0, zero, q)], axis=0)
    o_acc[...] = jnp.zeros_like(o_acc)
    c_acc[...] = jnp.zeros_like(c_acc)

    def step(j, strict_mask):
        start = pl.multiple_of(j * blk, blk)
        kt = k_ref[pl.ds(start, blk), :]
        vt = v_ref[pl.ds(start, blk), :]
        z = lax.dot_general(qq, kt, (((1,), (1,)), ((), ())), preferred_element_type=F32)
        soft = jnp.log1p(jnp.exp(-jnp.abs(z)))
        log_beta = jnp.minimum(z, 0.0) - soft
        log_keep = -jnp.maximum(z, 0.0) - soft
        if strict_mask is not None:
            log_keep = jnp.where(strict_mask, log_keep, 0.0)
        hi = log_keep.astype(BF16)
        lo = (log_keep - hi.astype(F32)).astype(BF16)
        cs = (jnp.dot(hi, cm_ref[...], preferred_element_type=F32)
              + jnp.dot(lo, cm_ref[...], preferred_element_type=F32))
        carry = c_acc[...]
        later = cs[:, 0:blk] + jnp.concatenate([carry] * (blk // LANES), axis=1)
        w = jnp.exp(log_beta + later)
        if strict_mask is not None:
            w = jnp.where(strict_mask, w, 0.0)
        o_acc[...] += jnp.dot(w.astype(BF16), vt, preferred_element_type=F32)
        c_acc[...] = carry + cs[:, blk:]

    row = lax.broadcasted_iota(jnp.int32, (2 * blk, blk), 0)
    col = lax.broadcasted_iota(jnp.int32, (2 * blk, blk), 1)
    step(i, col < jnp.where(row >= blk, row - blk, row))

    def body(n, _):
        step(i - 1 - n, None)
        return 0

    lax.fori_loop(0, i, body, 0)

    o = o_acc[...]
    o_ref[...] = jnp.where(head0, o[0:blk, :], o[blk:, :]).astype(BF16)


def _cumsum_matrix(blk):
    j = lax.broadcasted_iota(jnp.int32, (blk, blk + LANES), 0)
    s = lax.broadcasted_iota(jnp.int32, (blk, blk + LANES), 1)
    return jnp.where((j > s) | (s >= blk), 1.0, 0.0).astype(BF16)


def _sb_attention(qkv, batch, seq):
    blk = ATT_BLOCK
    nq = seq // blk
    npair = D_SB // LANES
    cm = _cumsum_matrix(blk)
    return pl.pallas_call(
        _sb_attn_kernel,
        grid=(batch, npair, nq),
        in_specs=[pl.BlockSpec((blk, LANES), lambda b, p, i: (b * nq + i, p)),
                  pl.BlockSpec((seq, LANES), lambda b, p, i: (b, npair + p)),
                  pl.BlockSpec((seq, LANES), lambda b, p, i: (b, 2 * npair + p)),
                  pl.BlockSpec(cm.shape, lambda b, p, i: (0, 0))],
        out_specs=pl.BlockSpec((blk, LANES), lambda b, p, i: (b * nq + i, p)),
        out_shape=jax.ShapeDtypeStruct((batch * seq, D_SB), BF16),
        scratch_shapes=[pltpu.VMEM((2 * blk, LANES), F32), pltpu.VMEM((2 * blk, LANES), F32)],
        compiler_params=_params("parallel", "parallel", "arbitrary"),
        name="stickbreak_attn",
    )(qkv, qkv, qkv, cm)


def _head_norm(y, g, avg):
    y2 = y * y
    hi = y2.astype(BF16)
    lo = (y2 - hi.astype(F32)).astype(BF16)
    ms = (jnp.dot(hi, avg, preferred_element_type=F32) + jnp.dot(lo, avg, preferred_element_type=F32))
    return (y * lax.rsqrt(ms + EPS) * g).astype(BF16)


def _outproj_kernel(ylru_ref, ysb_ref, ysc_ref, g_ref, avg_ref, w_ref, h_ref, o_ref):
    acc = h_ref[...]
    lo = 0
    for y_ref in (ylru_ref, ysb_ref, ysc_ref):
        c = y_ref.shape[1]
        yn = _head_norm(y_ref[...].astype(F32), g_ref[:, lo:lo + c], avg_ref[0:c, 0:c])
        acc = acc + jnp.dot(yn, w_ref[lo:lo + c, :], preferred_element_type=F32)
        lo += c
    o_ref[...] = acc


def _outproj(ylru, ysb, ysc, g, w, h):
    t = h.shape[0]
    r = lax.broadcasted_iota(jnp.int32, (D_LRU, D_LRU), 0) // HEAD_DIM
    c = lax.broadcasted_iota(jnp.int32, (D_LRU, D_LRU), 1) // HEAD_DIM
    avg = jnp.where(r == c, 1.0 / HEAD_DIM, 0.0).astype(BF16)
    row = lambda cdim: pl.BlockSpec((ROW_TILE, cdim), lambda i: (i, 0))
    full = lambda a: pl.BlockSpec(a.shape, lambda i: (0, 0))
    return pl.pallas_call(
        _outproj_kernel,
        grid=(t // ROW_TILE,),
        in_specs=[row(D_LRU), row(D_SB), row(D_SC), full(g), full(avg), full(w), row(D_MODEL)],
        out_specs=row(D_MODEL),
        out_shape=jax.ShapeDtypeStruct((t, D_MODEL), F32),
        compiler_params=_params("parallel"),
        name="headnorm_outproj",
    )(ylru, ysb, ysc, g, avg, w, h)


def _swiglu_partial(xn, wg, wu, wd, row_scale=None):
    a = jnp.dot(xn, wg, preferred_element_type=F32)
    b = jnp.dot(xn, wu, preferred_element_type=F32)
    hid = a * _sigmoid(a) * b
    if row_scale is not None:
        hid = hid * row_scale
    return jnp.dot(hid.astype(BF16), wd, preferred_element_type=F32)


def _ffn_kernel(h_ref, g_ref, wg_ref, wu_ref, wd_ref, o_ref, xn_ref, acc_ref):
    f = pl.program_id(1)

    @pl.when(f == 0)
    def _():
        xn_ref[...] = _rms_norm_rows(h_ref[...], g_ref[...]).astype(BF16)
        acc_ref[...] = jnp.zeros_like(acc_ref)

    acc_ref[...] += _swiglu_partial(xn_ref[...], wg_ref[...], wu_ref[...], wd_ref[...])

    @pl.when(f == pl.num_programs(1) - 1)
    def _():
        o_ref[...] = h_ref[...] + acc_ref[...]


def _ffn(h, g, wg, wu, wd):
    t = h.shape[0]
    nf = D_FF // FF_TILE
    return pl.pallas_call(
        _ffn_kernel,
        grid=(t // ROW_TILE, nf),
        in_specs=[pl.BlockSpec((ROW_TILE, D_MODEL), lambda i, f: (i, 0)),
                  pl.BlockSpec(g.shape, lambda i, f: (0, 0)),
                  pl.BlockSpec((D_MODEL, FF_TILE), lambda i, f: (0, f)),
                  pl.BlockSpec((D_MODEL, FF_TILE), lambda i, f: (0, f)),
                  pl.BlockSpec((FF_TILE, D_MODEL), lambda i, f: (f, 0))],
        out_specs=pl.BlockSpec((ROW_TILE, D_MODEL), lambda i, f: (i, 0)),
        out_shape=jax.ShapeDtypeStruct((t, D_MODEL), F32),
        scratch_shapes=[pltpu.VMEM((ROW_TILE, D_MODEL), BF16), pltpu.VMEM((ROW_TILE, D_MODEL), F32)],
        compiler_params=_params("parallel", "arbitrary"),
        name="norm_swiglu",
    )(h, g, wg, wu, wd)


def _top2_gates(logits):
    lane = lax.broadcasted_iota(jnp.int32, logits.shape, 1)
    m1 = jnp.max(logits, axis=-1, keepdims=True)
    i1 = jnp.min(jnp.where(logits == m1, lane, N_EXPERTS), axis=-1, keepdims=True)
    first = lane == i1
    rest = jnp.where(first, -jnp.inf, logits)
    m2 = jnp.max(rest, axis=-1, keepdims=True)
    i2 = jnp.min(jnp.where(rest == m2, lane, N_EXPERTS), axis=-1, keepdims=True)
    e2 = jnp.exp(m2 - m1)
    w1 = 1.0 / (1.0 + e2)
    return jnp.where(first, w1, jnp.where(lane == i2, e2 * w1, 0.0))


def _router_logits(xn_f32, wr_hi, wr_lo):
    x_hi = xn_f32.astype(BF16)
    x_lo = (xn_f32 - x_hi.astype(F32)).astype(BF16)
    return (jnp.dot(x_hi, wr_hi, preferred_element_type=F32)
            + jnp.dot(x_hi, wr_lo, preferred_element_type=F32)
            + jnp.dot(x_lo, wr_hi, preferred_element_type=F32))


def _moe_kernel(h_ref, g_ref, wrh_ref, wrl_ref, wg_ref, wu_ref, wd_ref, o_ref,
                xn_ref, acc_ref, gate_ref):
    e = pl.program_id(1)
    f = pl.program_id(2)

    @pl.when((e == 0) & (f == 0))
    def _():
        xn = _rms_norm_rows(h_ref[...], g_ref[...])
        xn_ref[...] = xn.astype(BF16)
        gate_ref[...] = _top2_gates(_router_logits(xn, wrh_ref[...], wrl_ref[...]))
        acc_ref[...] = jnp.zeros_like(acc_ref)

    gates = gate_ref[...]
    lane = lax.broadcasted_iota(jnp.int32, gates.shape, 1)
    gate_e = jnp.sum(jnp.where(lane == e, gates, 0.0), axis=-1, keepdims=True)
    acc_ref[...] += _swiglu_partial(xn_ref[...], wg_ref[...], wu_ref[...], wd_ref[...], gate_e)

    @pl.when((e == pl.num_programs(1) - 1) & (f == pl.num_programs(2) - 1))
    def _():
        o_ref[...] = h_ref[...] + acc_ref[...]


def _moe(h, g, wr, wg, wu, wd):
    t = h.shape[0]
    nf = D_FF // FF_TILE
    wr_hi = wr.astype(BF16)
    wr_lo = (wr - wr_hi.astype(F32)).astype(BF16)
    return pl.pallas_call(
        _moe_kernel,
        grid=(t // ROW_TILE, N_EXPERTS, nf),
        in_specs=[pl.BlockSpec((ROW_TILE, D_MODEL), lambda i, e, f: (i, 0)),
                  pl.BlockSpec(g.shape, lambda i, e, f: (0, 0)),
                  pl.BlockSpec(wr_hi.shape, lambda i, e, f: (0, 0)),
                  pl.BlockSpec(wr_lo.shape, lambda i, e, f: (0, 0)),
                  pl.BlockSpec((None, D_MODEL, FF_TILE), lambda i, e, f: (e, 0, f)),
                  pl.BlockSpec((None, D_MODEL, FF_TILE), lambda i, e, f: (e, 0, f)),
                  pl.BlockSpec((None, FF_TILE, D_MODEL), lambda i, e, f: (e, f, 0))],
        out_specs=pl.BlockSpec((ROW_TILE, D_MODEL), lambda i, e, f: (i, 0)),
        out_shape=jax.ShapeDtypeStruct((t, D_MODEL), F32),
        scratch_shapes=[pltpu.VMEM((ROW_TILE, D_MODEL), BF16), pltpu.VMEM((ROW_TILE, D_MODEL), F32),
                        pltpu.VMEM((ROW_TILE, N_EXPERTS), F32)],
        compiler_params=_params("parallel", "arbitrary", "arbitrary"),
        name="norm_moe_swiglu",
    )(h, g, wr_hi, wr_lo, wg, wu, wd)


def _final_norm_kernel(h_ref, g_ref, o_ref):
    o_ref[...] = _rms_norm_rows(h_ref[...], g_ref[...])


def _final_norm(h, g):
    t = h.shape[0]
    row = pl.BlockSpec((ROW_TILE, D_MODEL), lambda i: (i, 0))
    return pl.pallas_call(
        _final_norm_kernel,
        grid=(t // ROW_TILE,),
        in_specs=[row, pl.BlockSpec(g.shape, lambda i: (0, 0))],
        out_specs=row,
        out_shape=jax.ShapeDtypeStruct((t, D_MODEL), F32),
        compiler_params=_params("parallel"),
        name="final_norm",
    )(h, g)


def _block_diag(w):
    n = w.shape[0]
    eye = jnp.eye(n, dtype=w.dtype)
    return jnp.einsum("hij,hg->higj", w, eye).reshape(n * HEAD_DIM, n * HEAD_DIM)


def kernel(x, mix_norm_g, w_in, lru_conv_w, lru_conv_b, lru_wa, lru_ba, lru_wx, lru_bx, lru_lam,
           sc_conv_w, mix_out_g, w_out, ffn_norm_g, dense_wg, dense_wu, dense_wd,
           router_w, moe_wg, moe_wu, moe_wd, final_norm_g):
    batch, seq, _ = x.shape
    depth = w_in.shape[0]
    assert seq % ATT_BLOCK == 0 and seq % LRU_CHUNK == 0 and (batch * seq) % ROW_TILE == 0
    h = x.reshape(batch * seq, D_MODEL)
    row = lambda v: v.reshape(1, -1)
    for l in range(depth):
        lru, qkv, sc = _inproj(h, row(mix_norm_g[l]), w_in[l].astype(BF16))
        wgate = jnp.concatenate([_block_diag(lru_wa[l]), _block_diag(lru_wx[l])], axis=1).astype(BF16)
        bgate = jnp.concatenate([lru_ba[l], lru_bx[l]]).reshape(1, -1)
        y_lru, y_sc = _lru_sc(lru, sc, lru_conv_w[l], row(lru_conv_b[l]), wgate, bgate,
                              row(lru_lam[l]), sc_conv_w[l], batch, seq)
        y_sb = _sb_attention(qkv, batch, seq)
        h = _outproj(y_lru, y_sb, y_sc, row(mix_out_g[l]), w_out[l].astype(BF16), h)
        j = l // 2
        if l % 2 == 0:
            h = _ffn(h, row(ffn_norm_g[l]), dense_wg[j].astype(BF16), dense_wu[j].astype(BF16),
                     dense_wd[j].astype(BF16))
        else:
            h = _moe(h, row(ffn_norm_g[l]), router_w[j], moe_wg[j].astype(BF16),
                     moe_wu[j].astype(BF16), moe_wd[j].astype(BF16))
    return _final_norm(h, row(final_norm_g)).reshape(batch, seq, D_MODEL)
```

```python
import functools

import jax
import jax.numpy as jnp
from jax import lax
from jax.experimental import pallas as pl
from jax.experimental.pallas import tpu as pltpu

F32 = jnp.float32
BF16 = jnp.bfloat16

D_MODEL = 1024
HEAD_DIM = 64
D_LRU = 384
D_SB = 384
D_SC = 256
D_MIX = D_LRU + D_SB + D_SC
LRU_CONV = 4
SC_CONV = 3
LRU_C = 8.0
D_FF = 2816
N_EXPERTS = 8
EPS = 1e-6
F32_EXP_ZERO = -104.0

V7X_VMEM_LIMIT_BYTES = 56 * 1024 * 1024
SUBLANES = 8
LANES = 128

ROW_TILE = 512
LRU_CHUNK = 256
ATT_BLOCK = 256
FF_TILE = 1408
MOE_TILE = 2048
MOE_SUB = 512
MOE_CHUNK = 256


def _params(*sem):
    return pltpu.CompilerParams(dimension_semantics=sem, vmem_limit_bytes=V7X_VMEM_LIMIT_BYTES)


def _rms_norm_rows(x, g):
    ms = jnp.mean(x * x, axis=-1, keepdims=True)
    return x * lax.rsqrt(ms + EPS) * g


def _sigmoid(x):
    return 1.0 / (1.0 + jnp.exp(-x))


def _inproj_kernel(h_ref, g_ref, w_ref, lru_ref, qkv_ref, sc_ref):
    xn = _rms_norm_rows(h_ref[...], g_ref[...]).astype(BF16)
    c0, c1 = 2 * D_LRU, 2 * D_LRU + 3 * D_SB
    lru_ref[...] = jnp.dot(xn, w_ref[:, 0:c0], preferred_element_type=F32).astype(BF16)
    qkv_ref[...] = jnp.dot(xn, w_ref[:, c0:c1], preferred_element_type=F32).astype(BF16)
    sc_ref[...] = jnp.dot(xn, w_ref[:, c1:], preferred_element_type=F32).astype(BF16)


def _inproj(h, g, w):
    t = h.shape[0]
    d_in = w.shape[1]
    row = lambda c: pl.BlockSpec((ROW_TILE, c), lambda i: (i, 0))
    full = lambda a: pl.BlockSpec(a.shape, lambda i: (0, 0))
    return pl.pallas_call(
        _inproj_kernel,
        grid=(t // ROW_TILE,),
        in_specs=[row(D_MODEL), full(g), full(w)],
        out_specs=[row(2 * D_LRU), row(3 * D_SB), row(3 * D_SC)],
        out_shape=[jax.ShapeDtypeStruct((t, 2 * D_LRU), BF16),
                   jax.ShapeDtypeStruct((t, 3 * D_SB), BF16),
                   jax.ShapeDtypeStruct((t, d_in - 2 * D_LRU - 3 * D_SB), BF16)],
        compiler_params=_params("parallel"),
        name="norm_inproj",
    )(h, g, w)


def _lru_sc_kernel(lru_ref, sc_ref, cw_ref, cb_ref, wgate_ref, bgate_ref, lam_ref, scw_ref,
                   ylru_ref, ysc_ref, xbuf, pbuf, abuf, bbuf, hcar):
    tc = lru_ref.shape[0]
    pad = tc // 2

    @pl.when(pl.program_id(1) == 0)
    def _():
        xbuf[0:SUBLANES, :] = jnp.zeros((SUBLANES, D_LRU), F32)
        pbuf[0:SUBLANES, :] = jnp.zeros((SUBLANES, D_SC), F32)
        hcar[...] = jnp.zeros_like(hcar)
        abuf[0:pad, :] = jnp.ones((pad, D_LRU), F32)
        bbuf[0:pad, :] = jnp.zeros((pad, D_LRU), F32)

    xbuf[SUBLANES:SUBLANES + tc, :] = lru_ref[:, 0:D_LRU].astype(F32)
    conv = cb_ref[...]
    for k in range(LRU_CONV):
        off = SUBLANES - (LRU_CONV - 1) + k
        conv = conv + cw_ref[k:k + 1, :] * xbuf[pl.ds(off, tc), :]
    xbuf[0:SUBLANES, :] = xbuf[tc:tc + SUBLANES, :]

    gates = jnp.dot(conv.astype(BF16), wgate_ref[...], preferred_element_type=F32) + bgate_ref[...]
    gate_r = _sigmoid(gates[:, 0:D_LRU])
    gate_i = _sigmoid(gates[:, D_LRU:])
    lam = lam_ref[...]
    log_sig_lam = jnp.minimum(lam, 0.0) - jnp.log1p(jnp.exp(-jnp.abs(lam)))
    log_a = (LRU_C * gate_r) * log_sig_lam
    a = jnp.exp(log_a)
    u = jnp.sqrt(1.0 - a * a) * (gate_i * conv)

    s = 1
    while s < tc:
        abuf[pad:pad + tc, :] = a
        bbuf[pad:pad + tc, :] = u
        a_sh = abuf[pl.ds(pad - s, tc), :]
        u_sh = bbuf[pl.ds(pad - s, tc), :]
        u = a * u_sh + u
        a = a * a_sh
        s *= 2
    h = a * hcar[0:1, :] + u
    hcar[...] = jnp.broadcast_to(h[tc - 1:tc, :], hcar.shape)

    ylru_ref[...] = (h * jax.nn.gelu(lru_ref[:, D_LRU:].astype(F32))).astype(BF16)

    sc_b = sc_ref[:, 0:D_SC].astype(F32)
    pbuf[SUBLANES:SUBLANES + tc, :] = (sc_ref[:, D_SC:2 * D_SC].astype(F32)
                                       * sc_ref[:, 2 * D_SC:].astype(F32))
    acc = jnp.zeros((tc, D_SC), F32)
    for k in range(SC_CONV):
        off = SUBLANES - (SC_CONV - 1) + k
        acc = acc + scw_ref[k:k + 1, :] * pbuf[pl.ds(off, tc), :]
    pbuf[0:SUBLANES, :] = pbuf[tc:tc + SUBLANES, :]
    ysc_ref[...] = (sc_b * acc).astype(BF16)


def _lru_sc(lru, sc, cw, cb, wgate, bgate, lam, scw, batch, seq):
    tc = LRU_CHUNK
    nt = seq // tc
    row = lambda c: pl.BlockSpec((tc, c), lambda b, t: (b * nt + t, 0))
    full = lambda a: pl.BlockSpec(a.shape, lambda b, t: (0, 0))
    return pl.pallas_call(
        _lru_sc_kernel,
        grid=(batch, nt),
        in_specs=[row(2 * D_LRU), row(3 * D_SC), full(cw), full(cb), full(wgate), full(bgate),
                  full(lam), full(scw)],
        out_specs=[row(D_LRU), row(D_SC)],
        out_shape=[jax.ShapeDtypeStruct((batch * seq, D_LRU), BF16),
                   jax.ShapeDtypeStruct((batch * seq, D_SC), BF16)],
        scratch_shapes=[pltpu.VMEM((tc + SUBLANES, D_LRU), F32),
                        pltpu.VMEM((tc + SUBLANES, D_SC), F32),
                        pltpu.VMEM((tc + tc // 2, D_LRU), F32),
                        pltpu.VMEM((tc + tc // 2, D_LRU), F32),
                        pltpu.VMEM((SUBLANES, D_LRU), F32)],
        compiler_params=_params("parallel", "arbitrary"),
        name="lru_shortconv",
    )(lru, sc, cw, cb, wgate, bgate, lam, scw)


def _sb_attn_kernel(q_ref, k_ref, v_ref, cm_ref, o_ref, o_acc, c_acc):
    blk = q_ref.shape[0]
    i = pl.program_id(2)
    head0 = lax.broadcasted_iota(jnp.int32, (1, LANES), 1) < HEAD_DIM

    q = q_ref[...] * jnp.asarray(HEAD_DIM ** -0.5, BF16)
    zero = jnp.zeros_like(q)
    qq = jnp.concatenate([jnp.where(head0, q, zero), jnp.where(head0, zero, q)], axis=0)
    o_acc[...] = jnp.zeros_like(o_acc)
    c_acc[...] = jnp.zeros_like(c_acc)

    def step(j, strict_mask):
        start = pl.multiple_of(j * blk, blk)
        kt = k_ref[pl.ds(start, blk), :]
        vt = v_ref[pl.ds(start, blk), :]
        z = lax.dot_general(qq, kt, (((1,), (1,)), ((), ())), preferred_element_type=F32)
        neg_abs = lax.bitcast_convert_type(
            lax.bitcast_convert_type(z, jnp.uint32) | jnp.uint32(0x80000000), F32)
        soft = jnp.log(1.0 + jnp.exp(neg_abs))
        log_beta = jnp.minimum(z, 0.0) - soft
        log_keep = log_beta - z
        if strict_mask is not None:
            log_keep = jnp.where(strict_mask, log_keep, 0.0)
        cs = jnp.dot(log_keep.astype(BF16), cm_ref[...], preferred_element_type=F32)
        carry = c_acc[...]
        later = cs[:, 0:blk] + jnp.concatenate([carry] * (blk // LANES), axis=1)
        w = jnp.exp(log_beta + later)
        if strict_mask is not None:
            w = jnp.where(strict_mask, w, 0.0)
        o_acc[...] += jnp.dot(w.astype(BF16), vt, preferred_element_type=F32)
        carry = carry + cs[:, blk:]
        c_acc[...] = carry
        return jnp.max(carry)

    row = lax.broadcasted_iota(jnp.int32, (2 * blk, blk), 0)
    col = lax.broadcasted_iota(jnp.int32, (2 * blk, blk), 1)
    live = step(i, col < jnp.where(row >= blk, row - blk, row))

    def cond(state):
        n, live = state
        return jnp.logical_and(n < i, live > F32_EXP_ZERO)

    def body(state):
        n, _ = state
        return n + 1, step(i - 1 - n, None)

    lax.while_loop(cond, body, (jnp.int32(0), live))

    o = o_acc[...]
    o_ref[...] = jnp.where(head0, o[0:blk, :], o[blk:, :]).astype(BF16)


def _cumsum_matrix(blk):
    j = lax.broadcasted_iota(jnp.int32, (blk, blk + LANES), 0)
    s = lax.broadcasted_iota(jnp.int32, (blk, blk + LANES), 1)
    return jnp.where((j > s) | (s >= blk), 1.0, 0.0).astype(BF16)


def _sb_attention(qkv, batch, seq):
    blk = ATT_BLOCK
    nq = seq // blk
    npair = D_SB // LANES
    cm = _cumsum_matrix(blk)
    return pl.pallas_call(
        _sb_attn_kernel,
        grid=(batch, npair, nq),
        in_specs=[pl.BlockSpec((blk, LANES), lambda b, p, i: (b * nq + i, p)),
                  pl.BlockSpec((seq, LANES), lambda b, p, i: (b, npair + p)),
                  pl.BlockSpec((seq, LANES), lambda b, p, i: (b, 2 * npair + p)),
                  pl.BlockSpec(cm.shape, lambda b, p, i: (0, 0))],
        out_specs=pl.BlockSpec((blk, LANES), lambda b, p, i: (b * nq + i, p)),
        out_shape=jax.ShapeDtypeStruct((batch * seq, D_SB), BF16),
        scratch_shapes=[pltpu.VMEM((2 * blk, LANES), F32), pltpu.VMEM((2 * blk, LANES), F32)],
        compiler_params=_params("parallel", "parallel", "arbitrary"),
        name="stickbreak_attn",
    )(qkv, qkv, qkv, cm)


def _head_norm(y, g, avg):
    y2 = y * y
    hi = y2.astype(BF16)
    lo = (y2 - hi.astype(F32)).astype(BF16)
    ms = (jnp.dot(hi, avg, preferred_element_type=F32) + jnp.dot(lo, avg, preferred_element_type=F32))
    return (y * lax.rsqrt(ms + EPS) * g).astype(BF16)


def _outproj_kernel(ylru_ref, ysb_ref, ysc_ref, g_ref, avg_ref, w_ref, h_ref, o_ref):
    acc = h_ref[...]
    lo = 0
    for y_ref in (ylru_ref, ysb_ref, ysc_ref):
        c = y_ref.shape[1]
        yn = _head_norm(y_ref[...].astype(F32), g_ref[:, lo:lo + c], avg_ref[0:c, 0:c])
        acc = acc + jnp.dot(yn, w_ref[lo:lo + c, :], preferred_element_type=F32)
        lo += c
    o_ref[...] = acc


def _outproj(ylru, ysb, ysc, g, w, h):
    t = h.shape[0]
    r = lax.broadcasted_iota(jnp.int32, (D_LRU, D_LRU), 0) // HEAD_DIM
    c = lax.broadcasted_iota(jnp.int32, (D_LRU, D_LRU), 1) // HEAD_DIM
    avg = jnp.where(r == c, 1.0 / HEAD_DIM, 0.0).astype(BF16)
    row = lambda cdim: pl.BlockSpec((ROW_TILE, cdim), lambda i: (i, 0))
    full = lambda a: pl.BlockSpec(a.shape, lambda i: (0, 0))
    return pl.pallas_call(
        _outproj_kernel,
        grid=(t // ROW_TILE,),
        in_specs=[row(D_LRU), row(D_SB), row(D_SC), full(g), full(avg), full(w), row(D_MODEL)],
        out_specs=row(D_MODEL),
        out_shape=jax.ShapeDtypeStruct((t, D_MODEL), F32),
        compiler_params=_params("parallel"),
        name="headnorm_outproj",
    )(ylru, ysb, ysc, g, avg, w, h)


def _swiglu_partial(xn, wg, wu, wd, row_scale=None):
    a = jnp.dot(xn, wg, preferred_element_type=F32)
    b = jnp.dot(xn, wu, preferred_element_type=F32)
    hid = a * _sigmoid(a) * b
    if row_scale is not None:
        hid = hid * row_scale
    return jnp.dot(hid.astype(BF16), wd, preferred_element_type=F32)


def _ffn_kernel(h_ref, g_ref, wg_ref, wu_ref, wd_ref, o_ref, xn_ref, acc_ref):
    f = pl.program_id(1)

    @pl.when(f == 0)
    def _():
        xn_ref[...] = _rms_norm_rows(h_ref[...], g_ref[...]).astype(BF16)
        acc_ref[...] = jnp.zeros_like(acc_ref)

    acc_ref[...] += _swiglu_partial(xn_ref[...], wg_ref[...], wu_ref[...], wd_ref[...])

    @pl.when(f == pl.num_programs(1) - 1)
    def _():
        o_ref[...] = h_ref[...] + acc_ref[...]


def _ffn(h, g, wg, wu, wd):
    t = h.shape[0]
    nf = D_FF // FF_TILE
    return pl.pallas_call(
        _ffn_kernel,
        grid=(t // ROW_TILE, nf),
        in_specs=[pl.BlockSpec((ROW_TILE, D_MODEL), lambda i, f: (i, 0)),
                  pl.BlockSpec(g.shape, lambda i, f: (0, 0)),
                  pl.BlockSpec((D_MODEL, FF_TILE), lambda i, f: (0, f)),
                  pl.BlockSpec((D_MODEL, FF_TILE), lambda i, f: (0, f)),
                  pl.BlockSpec((FF_TILE, D_MODEL), lambda i, f: (f, 0))],
        out_specs=pl.BlockSpec((ROW_TILE, D_MODEL), lambda i, f: (i, 0)),
        out_shape=jax.ShapeDtypeStruct((t, D_MODEL), F32),
        scratch_shapes=[pltpu.VMEM((ROW_TILE, D_MODEL), BF16), pltpu.VMEM((ROW_TILE, D_MODEL), F32)],
        compiler_params=_params("parallel", "arbitrary"),
        name="norm_swiglu",
    )(h, g, wg, wu, wd)


def _top2_gates(logits):
    lane = lax.broadcasted_iota(jnp.int32, logits.shape, 1)
    m1 = jnp.max(logits, axis=-1, keepdims=True)
    i1 = jnp.min(jnp.where(logits == m1, lane, N_EXPERTS), axis=-1, keepdims=True)
    first = lane == i1
    rest = jnp.where(first, -jnp.inf, logits)
    m2 = jnp.max(rest, axis=-1, keepdims=True)
    i2 = jnp.min(jnp.where(rest == m2, lane, N_EXPERTS), axis=-1, keepdims=True)
    e2 = jnp.exp(m2 - m1)
    w1 = 1.0 / (1.0 + e2)
    return jnp.where(first, w1, jnp.where(lane == i2, e2 * w1, 0.0))


def _router_logits(xn_f32, wr_hi, wr_lo):
    x_hi = xn_f32.astype(BF16)
    x_lo = (xn_f32 - x_hi.astype(F32)).astype(BF16)
    return (jnp.dot(x_hi, wr_hi, preferred_element_type=F32)
            + jnp.dot(x_hi, wr_lo, preferred_element_type=F32)
            + jnp.dot(x_lo, wr_hi, preferred_element_type=F32))


def _router_kernel(h_ref, g_ref, wrh_ref, wrl_ref, ltri_ref, xn_ref, gate_ref, rank_ref, cum_ref,
                   base_ref, *, subs_per_tile):
    @pl.when(pl.program_id(0) % subs_per_tile == 0)
    def _():
        base_ref[...] = jnp.zeros_like(base_ref)

    xn = _rms_norm_rows(h_ref[...], g_ref[...])
    xn_ref[...] = xn.astype(BF16)
    gates = _top2_gates(_router_logits(xn, wrh_ref[...], wrl_ref[...]))
    gate_ref[...] = gates
    sel = gates > 0.0
    inc = jnp.dot(ltri_ref[...], jnp.where(sel, 1.0, 0.0).astype(BF16), preferred_element_type=F32)
    count = base_ref[...] + inc
    rank_ref[...] = jnp.where(sel, count - 1.0, -1.0).astype(jnp.int32)
    total = count[MOE_SUB - 1:MOE_SUB, :]
    base_ref[...] = total
    cum_ref[...] = total.astype(jnp.int32)


def _router(h, g, wr, moe_tile):
    t = h.shape[0]
    wr_hi = wr.astype(BF16)
    wr_lo = (wr - wr_hi.astype(F32)).astype(BF16)
    r = lax.broadcasted_iota(jnp.int32, (MOE_SUB, MOE_SUB), 0)
    c = lax.broadcasted_iota(jnp.int32, (MOE_SUB, MOE_SUB), 1)
    ltri = jnp.where(c <= r, 1.0, 0.0).astype(BF16)
    nsteps = t // MOE_SUB
    full = lambda a: pl.BlockSpec(a.shape, lambda i: (0, 0))
    row = lambda cdim: pl.BlockSpec((MOE_SUB, cdim), lambda i: (i, 0))
    return pl.pallas_call(
        functools.partial(_router_kernel, subs_per_tile=moe_tile // MOE_SUB),
        grid=(nsteps,),
        in_specs=[row(D_MODEL), full(g), full(wr_hi), full(wr_lo), full(ltri)],
        out_specs=[row(D_MODEL), row(N_EXPERTS), row(N_EXPERTS),
                   pl.BlockSpec((None, 1, N_EXPERTS), lambda i: (i, 0, 0))],
        out_shape=[jax.ShapeDtypeStruct((t, D_MODEL), BF16),
                   jax.ShapeDtypeStruct((t, N_EXPERTS), F32),
                   jax.ShapeDtypeStruct((t, N_EXPERTS), jnp.int32),
                   jax.ShapeDtypeStruct((nsteps, 1, N_EXPERTS), jnp.int32)],
        scratch_shapes=[pltpu.VMEM((1, N_EXPERTS), F32)],
        compiler_params=_params("arbitrary"),
        name="norm_router",
    )(h, g, wr_hi, wr_lo, ltri)


def _moe_kernel(cum_ref, x_ref, rank_ref, gate_ref, wg_ref, wu_ref, wd_ref, o_ref,
                xc_ref, yc_ref, tmp_ref):
    i, e, f = pl.program_id(0), pl.program_id(1), pl.program_id(2)
    tile = x_ref.shape[0]
    nsub = tile // MOE_SUB
    base = (i * N_EXPERTS + e) * (nsub + 1)
    nchunk = (cum_ref[base + nsub] + MOE_CHUNK - 1) // MOE_CHUNK

    @pl.when((e == 0) & (f == 0))
    def _():
        o_ref[...] = jnp.zeros_like(o_ref)

    def overlaps(c, s):
        return (cum_ref[base + s] < (c + 1) * MOE_CHUNK) & (cum_ref[base + s + 1] > c * MOE_CHUNK)

    def match(c, s):
        rows = c * MOE_CHUNK + lax.broadcasted_iota(jnp.int32, (MOE_CHUNK, MOE_SUB), 0)
        return rank_ref[:, s * MOE_SUB:(s + 1) * MOE_SUB] == rows

    @pl.when(f == 0)
    def _():
        def gather(c, _):
            tmp_ref[...] = jnp.zeros_like(tmp_ref)
            for s in range(nsub):
                @pl.when(overlaps(c, s))
                def _():
                    p = jnp.where(match(c, s), 1.0, 0.0).astype(BF16)
                    tmp_ref[...] += jnp.dot(p, x_ref[s * MOE_SUB:(s + 1) * MOE_SUB, :],
                                            preferred_element_type=F32)
            xc_ref[pl.ds(pl.multiple_of(c * MOE_CHUNK, MOE_CHUNK), MOE_CHUNK), :] = (
                tmp_ref[...].astype(BF16))
            return 0
        lax.fori_loop(0, nchunk, gather, 0)

    def ffn(c, _):
        rows = pl.ds(pl.multiple_of(c * MOE_CHUNK, MOE_CHUNK), MOE_CHUNK)
        y = _swiglu_partial(xc_ref[rows, :], wg_ref[...], wu_ref[...], wd_ref[...])

        @pl.when(f == 0)
        def _():
            yc_ref[rows, :] = y

        @pl.when(f != 0)
        def _():
            yc_ref[rows, :] += y
        return 0
    lax.fori_loop(0, nchunk, ffn, 0)

    @pl.when(f == pl.num_programs(2) - 1)
    def _():
        def scatter(c, _):
            y = yc_ref[pl.ds(pl.multiple_of(c * MOE_CHUNK, MOE_CHUNK), MOE_CHUNK), :].astype(BF16)
            for s in range(nsub):
                @pl.when(overlaps(c, s))
                def _():
                    gate = gate_ref[:, s * MOE_SUB:(s + 1) * MOE_SUB]
                    pg = jnp.where(match(c, s), gate, 0.0).astype(BF16)
                    upd = lax.dot_general(pg, y, (((0,), (0,)), ((), ())),
                                          preferred_element_type=F32)
                    rows = slice(s * MOE_SUB, (s + 1) * MOE_SUB)
                    o_ref[rows, :] = (o_ref[rows, :].astype(F32) + upd).astype(BF16)
            return 0
        lax.fori_loop(0, nchunk, scatter, 0)


def _moe(xn, gates, rank, cum_end, wg, wu, wd, moe_tile):
    t = xn.shape[0]
    nf = D_FF // FF_TILE
    ntile = t // moe_tile
    nsub = moe_tile // MOE_SUB
    to_rows = lambda a: a.reshape(ntile, moe_tile, N_EXPERTS).transpose(0, 2, 1).reshape(
        ntile, N_EXPERTS, 1, moe_tile)
    cum = cum_end.reshape(ntile, nsub, N_EXPERTS).transpose(0, 2, 1)
    cum = jnp.concatenate([jnp.zeros((ntile, N_EXPERTS, 1), jnp.int32), cum], axis=-1).reshape(-1)
    meta = pl.BlockSpec((None, None, 1, moe_tile), lambda i, e, f, cum: (i, e, 0, 0))
    grid_spec = pltpu.PrefetchScalarGridSpec(
        num_scalar_prefetch=1,
        grid=(ntile, N_EXPERTS, nf),
        in_specs=[pl.BlockSpec((moe_tile, D_MODEL), lambda i, e, f, cum: (i, 0)),
                  meta, meta,
                  pl.BlockSpec((None, D_MODEL, FF_TILE), lambda i, e, f, cum: (e, 0, f)),
                  pl.BlockSpec((None, D_MODEL, FF_TILE), lambda i, e, f, cum: (e, 0, f)),
                  pl.BlockSpec((None, FF_TILE, D_MODEL), lambda i, e, f, cum: (e, f, 0))],
        out_specs=pl.BlockSpec((moe_tile, D_MODEL), lambda i, e, f, cum: (i, 0)),
        scratch_shapes=[pltpu.VMEM((moe_tile, D_MODEL), BF16),
                        pltpu.VMEM((moe_tile, D_MODEL), F32),
                        pltpu.VMEM((MOE_CHUNK, D_MODEL), F32)])
    return pl.pallas_call(
        _moe_kernel,
        grid_spec=grid_spec,
        out_shape=jax.ShapeDtypeStruct((t, D_MODEL), BF16),
        compiler_params=_params("parallel", "arbitrary", "arbitrary"),
        name="moe_swiglu",
    )(cum, xn, to_rows(rank), to_rows(gates), wg, wu, wd)


def _residual_add_kernel(h_ref, y_ref, o_ref):
    o_ref[...] = h_ref[...] + y_ref[...].astype(F32)


def _residual_add(h, y):
    t = h.shape[0]
    row = pl.BlockSpec((ROW_TILE, D_MODEL), lambda i: (i, 0))
    return pl.pallas_call(
        _residual_add_kernel,
        grid=(t // ROW_TILE,),
        in_specs=[row, row],
        out_specs=row,
        out_shape=jax.ShapeDtypeStruct((t, D_MODEL), F32),
        compiler_params=_params("parallel"),
        name="residual_add",
    )(h, y)


def _final_norm_kernel(h_ref, y_ref, g_ref, o_ref):
    o_ref[...] = _rms_norm_rows(h_ref[...] + y_ref[...].astype(F32), g_ref[...])


def _final_norm(h, y, g):
    t = h.shape[0]
    row = pl.BlockSpec((ROW_TILE, D_MODEL), lambda i: (i, 0))
    return pl.pallas_call(
        _final_norm_kernel,
        grid=(t // ROW_TILE,),
        in_specs=[row, row, pl.BlockSpec(g.shape, lambda i: (0, 0))],
        out_specs=row,
        out_shape=jax.ShapeDtypeStruct((t, D_MODEL), F32),
        compiler_params=_params("parallel"),
        name="final_norm",
    )(h, y, g)


def _block_diag(w):
    n = w.shape[0]
    eye = jnp.eye(n, dtype=w.dtype)
    return jnp.einsum("hij,hg->higj", w, eye).reshape(n * HEAD_DIM, n * HEAD_DIM)


def kernel(x, mix_norm_g, w_in, lru_conv_w, lru_conv_b, lru_wa, lru_ba, lru_wx, lru_bx, lru_lam,
           sc_conv_w, mix_out_g, w_out, ffn_norm_g, dense_wg, dense_wu, dense_wd,
           router_w, moe_wg, moe_wu, moe_wd, final_norm_g):
    batch, seq, _ = x.shape
    depth = w_in.shape[0]
    assert seq % ATT_BLOCK == 0 and seq % LRU_CHUNK == 0 and (batch * seq) % ROW_TILE == 0
    h = x.reshape(batch * seq, D_MODEL)
    row = lambda v: v.reshape(1, -1)
    moe_tile = min(MOE_TILE, batch * seq)
    assert (batch * seq) % moe_tile == 0 and moe_tile % MOE_SUB == 0
    pending = None
    for l in range(depth):
        lru, qkv, sc = _inproj(h, row(mix_norm_g[l]), w_in[l].astype(BF16))
        wgate = jnp.concatenate([_block_diag(lru_wa[l]), _block_diag(lru_wx[l])], axis=1).astype(BF16)
        bgate = jnp.concatenate([lru_ba[l], lru_bx[l]]).reshape(1, -1)
        y_lru, y_sc = _lru_sc(lru, sc, lru_conv_w[l], row(lru_conv_b[l]), wgate, bgate,
                              row(lru_lam[l]), sc_conv_w[l], batch, seq)
        y_sb = _sb_attention(qkv, batch, seq)
        h = _outproj(y_lru, y_sb, y_sc, row(mix_out_g[l]), w_out[l].astype(BF16), h)
        j = l // 2
        if l % 2 == 0:
            h = _ffn(h, row(ffn_norm_g[l]), dense_wg[j].astype(BF16), dense_wu[j].astype(BF16),
                     dense_wd[j].astype(BF16))
            pending = None
        else:
            xn, gates, rank, cum_end = _router(h, row(ffn_norm_g[l]), router_w[j], moe_tile)
            pending = _moe(xn, gates, rank, cum_end, moe_wg[j].astype(BF16),
                           moe_wu[j].astype(BF16), moe_wd[j].astype(BF16), moe_tile)
            if l + 1 < depth:
                h = _residual_add(h, pending)
    if pending is None:
        pending = jnp.zeros(h.shape, BF16)
    return _final_norm(h, pending, row(final_norm_g)).reshape(batch, seq, D_MODEL)
```

```python
import functools

import jax
import jax.numpy as jnp
from jax import lax
from jax.experimental import pallas as pl
from jax.experimental.pallas import tpu as pltpu

F32 = jnp.float32
BF16 = jnp.bfloat16

D_MODEL = 1024
HEAD_DIM = 64
D_LRU = 384
D_SB = 384
D_SC = 256
D_MIX = D_LRU + D_SB + D_SC
LRU_CONV = 4
SC_CONV = 3
LRU_C = 8.0
D_FF = 2816
N_EXPERTS = 8
EPS = 1e-6
F32_EXP_ZERO = -104.0

V7X_VMEM_LIMIT_BYTES = 56 * 1024 * 1024
SUBLANES = 8
LANES = 128

ROW_TILE = 512
LRU_CHUNK = 256
ATT_BLOCK = 256
FF_TILE = 1408
MOE_TILE = 2048
MOE_SUB = 512
MOE_CHUNK = 256
FF_SUB = 256
MOE_FF_SUB = 704


def _params(*sem):
    return pltpu.CompilerParams(dimension_semantics=sem, vmem_limit_bytes=V7X_VMEM_LIMIT_BYTES)


def _rms_norm_rows(x, g):
    ms = jnp.mean(x * x, axis=-1, keepdims=True)
    return x * lax.rsqrt(ms + EPS) * g


def _sigmoid(x):
    return 1.0 / (1.0 + jnp.exp(-x))


def _inproj_kernel(h_ref, g_ref, w_ref, lru_ref, qkv_ref, sc_ref):
    xn = _rms_norm_rows(h_ref[...], g_ref[...]).astype(BF16)
    c0, c1 = 2 * D_LRU, 2 * D_LRU + 3 * D_SB
    lru_ref[...] = jnp.dot(xn, w_ref[:, 0:c0], preferred_element_type=F32).astype(BF16)
    qkv_ref[...] = jnp.dot(xn, w_ref[:, c0:c1], preferred_element_type=F32).astype(BF16)
    sc_ref[...] = jnp.dot(xn, w_ref[:, c1:], preferred_element_type=F32).astype(BF16)


def _inproj(h, g, w):
    t = h.shape[0]
    d_in = w.shape[1]
    row = lambda c: pl.BlockSpec((ROW_TILE, c), lambda i: (i, 0))
    full = lambda a: pl.BlockSpec(a.shape, lambda i: (0, 0))
    return pl.pallas_call(
        _inproj_kernel,
        grid=(t // ROW_TILE,),
        in_specs=[row(D_MODEL), full(g), full(w)],
        out_specs=[row(2 * D_LRU), row(3 * D_SB), row(3 * D_SC)],
        out_shape=[jax.ShapeDtypeStruct((t, 2 * D_LRU), BF16),
                   jax.ShapeDtypeStruct((t, 3 * D_SB), BF16),
                   jax.ShapeDtypeStruct((t, d_in - 2 * D_LRU - 3 * D_SB), BF16)],
        compiler_params=_params("parallel"),
        name="norm_inproj",
    )(h, g, w)


def _lru_sc_kernel(lru_ref, sc_ref, cw_ref, cb_ref, wgate_ref, bgate_ref, lam_ref, scw_ref,
                   ylru_ref, ysc_ref, xbuf, pbuf, abuf, bbuf, hcar):
    tc = lru_ref.shape[0]
    pad = tc // 2

    @pl.when(pl.program_id(1) == 0)
    def _():
        xbuf[0:SUBLANES, :] = jnp.zeros((SUBLANES, D_LRU), F32)
        pbuf[0:SUBLANES, :] = jnp.zeros((SUBLANES, D_SC), F32)
        hcar[...] = jnp.zeros_like(hcar)
        abuf[0:pad, :] = jnp.ones((pad, D_LRU), F32)
        bbuf[0:pad, :] = jnp.zeros((pad, D_LRU), F32)

    xbuf[SUBLANES:SUBLANES + tc, :] = lru_ref[:, 0:D_LRU].astype(F32)
    conv = cb_ref[...]
    for k in range(LRU_CONV):
        off = SUBLANES - (LRU_CONV - 1) + k
        conv = conv + cw_ref[k:k + 1, :] * xbuf[pl.ds(off, tc), :]
    xbuf[0:SUBLANES, :] = xbuf[tc:tc + SUBLANES, :]

    gates = jnp.dot(conv.astype(BF16), wgate_ref[...], preferred_element_type=F32) + bgate_ref[...]
    gate_r = _sigmoid(gates[:, 0:D_LRU])
    gate_i = _sigmoid(gates[:, D_LRU:])
    lam = lam_ref[...]
    log_sig_lam = jnp.minimum(lam, 0.0) - jnp.log1p(jnp.exp(-jnp.abs(lam)))
    log_a = (LRU_C * gate_r) * log_sig_lam
    a = jnp.exp(log_a)
    u = jnp.sqrt(1.0 - a * a) * (gate_i * conv)

    s = 1
    while s < tc:
        abuf[pad:pad + tc, :] = a
        bbuf[pad:pad + tc, :] = u
        a_sh = abuf[pl.ds(pad - s, tc), :]
        u_sh = bbuf[pl.ds(pad - s, tc), :]
        u = a * u_sh + u
        a = a * a_sh
        s *= 2
    h = a * hcar[0:1, :] + u
    hcar[...] = jnp.broadcast_to(h[tc - 1:tc, :], hcar.shape)

    ylru_ref[...] = (h * jax.nn.gelu(lru_ref[:, D_LRU:].astype(F32))).astype(BF16)

    sc_b = sc_ref[:, 0:D_SC].astype(F32)
    pbuf[SUBLANES:SUBLANES + tc, :] = (sc_ref[:, D_SC:2 * D_SC].astype(F32)
                                       * sc_ref[:, 2 * D_SC:].astype(F32))
    acc = jnp.zeros((tc, D_SC), F32)
    for k in range(SC_CONV):
        off = SUBLANES - (SC_CONV - 1) + k
        acc = acc + scw_ref[k:k + 1, :] * pbuf[pl.ds(off, tc), :]
    pbuf[0:SUBLANES, :] = pbuf[tc:tc + SUBLANES, :]
    ysc_ref[...] = (sc_b * acc).astype(BF16)


def _lru_sc(lru, sc, cw, cb, wgate, bgate, lam, scw, batch, seq):
    tc = LRU_CHUNK
    nt = seq // tc
    row = lambda c: pl.BlockSpec((tc, c), lambda b, t: (b * nt + t, 0))
    full = lambda a: pl.BlockSpec(a.shape, lambda b, t: (0, 0))
    return pl.pallas_call(
        _lru_sc_kernel,
        grid=(batch, nt),
        in_specs=[row(2 * D_LRU), row(3 * D_SC), full(cw), full(cb), full(wgate), full(bgate),
                  full(lam), full(scw)],
        out_specs=[row(D_LRU), row(D_SC)],
        out_shape=[jax.ShapeDtypeStruct((batch * seq, D_LRU), BF16),
                   jax.ShapeDtypeStruct((batch * seq, D_SC), BF16)],
        scratch_shapes=[pltpu.VMEM((tc + SUBLANES, D_LRU), F32),
                        pltpu.VMEM((tc + SUBLANES, D_SC), F32),
                        pltpu.VMEM((tc + tc // 2, D_LRU), F32),
                        pltpu.VMEM((tc + tc // 2, D_LRU), F32),
                        pltpu.VMEM((SUBLANES, D_LRU), F32)],
        compiler_params=_params("parallel", "arbitrary"),
        name="lru_shortconv",
    )(lru, sc, cw, cb, wgate, bgate, lam, scw)


def _sb_attn_kernel(q_ref, k_ref, v_ref, cm_ref, o_ref, o_acc, c_acc):
    blk = q_ref.shape[0]
    npair = q_ref.shape[1] // LANES
    i = pl.program_id(1)
    head0 = lax.broadcasted_iota(jnp.int32, (1, LANES), 1) < HEAD_DIM
    lanes = lambda p: slice(p * LANES, (p + 1) * LANES)

    def stacked_q(p):
        q = q_ref[:, lanes(p)] * jnp.asarray(HEAD_DIM ** -0.5, BF16)
        zero = jnp.zeros_like(q)
        return jnp.concatenate([jnp.where(head0, q, zero), jnp.where(head0, zero, q)], axis=0)

    def step(p, qq, j, o, carry, strict_mask=None, valid=None):
        start = pl.multiple_of(j * blk, blk)
        kt = k_ref[pl.ds(start, blk), lanes(p)]
        vt = v_ref[pl.ds(start, blk), lanes(p)]
        z = lax.dot_general(qq, kt, (((1,), (1,)), ((), ())), preferred_element_type=F32)
        neg_abs = lax.bitcast_convert_type(
            lax.bitcast_convert_type(z, jnp.uint32) | jnp.uint32(0x80000000), F32)
        soft = jnp.log(1.0 + jnp.exp(neg_abs))
        log_beta = jnp.minimum(z, 0.0) - soft
        log_keep = log_beta - z
        keep = strict_mask if valid is None else valid
        if keep is not None:
            log_keep = jnp.where(keep, log_keep, 0.0)
        cs = jnp.dot(log_keep.astype(BF16), cm_ref[...], preferred_element_type=F32)
        later = cs[:, 0:blk] + jnp.concatenate([carry] * (blk // LANES), axis=1)
        w = jnp.exp(log_beta + later)
        if keep is not None:
            w = jnp.where(keep, w, 0.0)
        o = o + jnp.dot(w.astype(BF16), vt, preferred_element_type=F32)
        return o, carry + cs[:, blk:]

    row = lax.broadcasted_iota(jnp.int32, (2 * blk, blk), 0)
    col = lax.broadcasted_iota(jnp.int32, (2 * blk, blk), 1)
    strict = col < jnp.where(row >= blk, row - blk, row)
    zeros = jnp.zeros((2 * blk, LANES), F32)
    qqs = [stacked_q(p) for p in range(npair)]
    live = None
    for p in range(npair):
        o, carry = step(p, qqs[p], i, zeros, zeros, strict_mask=strict)
        o, carry = step(p, qqs[p], jnp.maximum(i - 1, 0), o, carry, valid=i > 0)
        o_acc[p] = o
        c_acc[p] = carry
        top = jnp.max(carry)
        live = top if live is None else jnp.maximum(live, top)

    def cond(state):
        n, live = state
        return jnp.logical_and(n < i, live > F32_EXP_ZERO)

    def body(state):
        n, _ = state
        live = None
        for p in range(npair):
            o, carry = step(p, qqs[p], i - 1 - n, o_acc[p], c_acc[p])
            o_acc[p] = o
            c_acc[p] = carry
            top = jnp.max(carry)
            live = top if live is None else jnp.maximum(live, top)
        return n + 1, live

    lax.while_loop(cond, body, (jnp.int32(1), live))

    for p in range(npair):
        o = o_acc[p]
        o_ref[:, lanes(p)] = jnp.where(head0, o[0:blk, :], o[blk:, :]).astype(BF16)


def _cumsum_matrix(blk):
    j = lax.broadcasted_iota(jnp.int32, (blk, blk + LANES), 0)
    s = lax.broadcasted_iota(jnp.int32, (blk, blk + LANES), 1)
    return jnp.where((j > s) | (s >= blk), 1.0, 0.0).astype(BF16)


def _sb_attention(qkv, batch, seq):
    blk = ATT_BLOCK
    nq = seq // blk
    npair = D_SB // LANES
    cm = _cumsum_matrix(blk)
    return pl.pallas_call(
        _sb_attn_kernel,
        grid=(batch, nq),
        in_specs=[pl.BlockSpec((blk, D_SB), lambda b, i: (b * nq + i, 0)),
                  pl.BlockSpec((seq, D_SB), lambda b, i: (b, 1)),
                  pl.BlockSpec((seq, D_SB), lambda b, i: (b, 2)),
                  pl.BlockSpec(cm.shape, lambda b, i: (0, 0))],
        out_specs=pl.BlockSpec((blk, D_SB), lambda b, i: (b * nq + i, 0)),
        out_shape=jax.ShapeDtypeStruct((batch * seq, D_SB), BF16),
        scratch_shapes=[pltpu.VMEM((npair, 2 * blk, LANES), F32),
                        pltpu.VMEM((npair, 2 * blk, LANES), F32)],
        compiler_params=_params("parallel", "arbitrary"),
        name="stickbreak_attn",
    )(qkv, qkv, qkv, cm)


def _head_norm(y, g, avg):
    ms = jnp.dot((y * y).astype(BF16), avg, preferred_element_type=F32)
    return (y * lax.rsqrt(ms + EPS) * g).astype(BF16)


def _outproj_kernel(ylru_ref, ysb_ref, ysc_ref, g_ref, avg_ref, w_ref, h_ref, o_ref):
    acc = h_ref[...]
    lo = 0
    for y_ref in (ylru_ref, ysb_ref, ysc_ref):
        c = y_ref.shape[1]
        yn = _head_norm(y_ref[...].astype(F32), g_ref[:, lo:lo + c], avg_ref[0:c, 0:c])
        acc = acc + jnp.dot(yn, w_ref[lo:lo + c, :], preferred_element_type=F32)
        lo += c
    o_ref[...] = acc


def _outproj(ylru, ysb, ysc, g, w, h):
    t = h.shape[0]
    r = lax.broadcasted_iota(jnp.int32, (D_LRU, D_LRU), 0) // HEAD_DIM
    c = lax.broadcasted_iota(jnp.int32, (D_LRU, D_LRU), 1) // HEAD_DIM
    avg = jnp.where(r == c, 1.0 / HEAD_DIM, 0.0).astype(BF16)
    row = lambda cdim: pl.BlockSpec((ROW_TILE, cdim), lambda i: (i, 0))
    full = lambda a: pl.BlockSpec(a.shape, lambda i: (0, 0))
    return pl.pallas_call(
        _outproj_kernel,
        grid=(t // ROW_TILE,),
        in_specs=[row(D_LRU), row(D_SB), row(D_SC), full(g), full(avg), full(w), row(D_MODEL)],
        out_specs=row(D_MODEL),
        out_shape=jax.ShapeDtypeStruct((t, D_MODEL), F32),
        compiler_params=_params("parallel"),
        name="headnorm_outproj",
    )(ylru, ysb, ysc, g, avg, w, h)


def _swiglu_partial(xn, wg_ref, wu_ref, wd_ref, sub):
    width = wg_ref.shape[1]
    out = None
    for lo in range(0, width, sub):
        hi = min(lo + sub, width)
        a = jnp.dot(xn, wg_ref[:, lo:hi], preferred_element_type=F32)
        b = jnp.dot(xn, wu_ref[:, lo:hi], preferred_element_type=F32)
        hid = (a * _sigmoid(a) * b).astype(BF16)
        part = jnp.dot(hid, wd_ref[lo:hi, :], preferred_element_type=F32)
        out = part if out is None else out + part
    return out


def _ffn_kernel(h_ref, g_ref, wg_ref, wu_ref, wd_ref, o_ref):
    h = h_ref[...]
    xn = _rms_norm_rows(h, g_ref[...]).astype(BF16)
    o_ref[...] = h + _swiglu_partial(xn, wg_ref, wu_ref, wd_ref, FF_SUB)


def _ffn(h, g, wg, wu, wd):
    t = h.shape[0]
    row = pl.BlockSpec((ROW_TILE, D_MODEL), lambda i: (i, 0))
    full = lambda a: pl.BlockSpec(a.shape, lambda i: (0, 0))
    return pl.pallas_call(
        _ffn_kernel,
        grid=(t // ROW_TILE,),
        in_specs=[row, full(g), full(wg), full(wu), full(wd)],
        out_specs=row,
        out_shape=jax.ShapeDtypeStruct((t, D_MODEL), F32),
        compiler_params=_params("parallel"),
        name="norm_swiglu",
    )(h, g, wg, wu, wd)


def _top2_gates(logits):
    lane = lax.broadcasted_iota(jnp.int32, logits.shape, 1)
    m1 = jnp.max(logits, axis=-1, keepdims=True)
    i1 = jnp.min(jnp.where(logits == m1, lane, N_EXPERTS), axis=-1, keepdims=True)
    first = lane == i1
    rest = jnp.where(first, -jnp.inf, logits)
    m2 = jnp.max(rest, axis=-1, keepdims=True)
    i2 = jnp.min(jnp.where(rest == m2, lane, N_EXPERTS), axis=-1, keepdims=True)
    e2 = jnp.exp(m2 - m1)
    w1 = 1.0 / (1.0 + e2)
    return jnp.where(first, w1, jnp.where(lane == i2, e2 * w1, 0.0))


def _router_logits(xn_f32, wr_hi, wr_lo):
    x_hi = xn_f32.astype(BF16)
    x_lo = (xn_f32 - x_hi.astype(F32)).astype(BF16)
    return (jnp.dot(x_hi, wr_hi, preferred_element_type=F32)
            + jnp.dot(x_hi, wr_lo, preferred_element_type=F32)
            + jnp.dot(x_lo, wr_hi, preferred_element_type=F32))


def _router_kernel(h_ref, g_ref, wrh_ref, wrl_ref, ltri_ref, xn_ref, gate_ref, rank_ref, cum_ref,
                   base_ref, *, subs_per_tile):
    @pl.when(pl.program_id(0) % subs_per_tile == 0)
    def _():
        base_ref[...] = jnp.zeros_like(base_ref)

    xn = _rms_norm_rows(h_ref[...], g_ref[...])
    xn_ref[...] = xn.astype(BF16)
    gates = _top2_gates(_router_logits(xn, wrh_ref[...], wrl_ref[...]))
    gate_ref[...] = gates
    sel = gates > 0.0
    inc = jnp.dot(ltri_ref[...], jnp.where(sel, 1.0, 0.0).astype(BF16), preferred_element_type=F32)
    count = base_ref[...] + inc
    rank_ref[...] = jnp.where(sel, count - 1.0, -1.0).astype(jnp.int32)
    total = count[MOE_SUB - 1:MOE_SUB, :]
    base_ref[...] = total
    cum_ref[...] = total.astype(jnp.int32)


def _router(h, g, wr, moe_tile):
    t = h.shape[0]
    wr_hi = wr.astype(BF16)
    wr_lo = (wr - wr_hi.astype(F32)).astype(BF16)
    r = lax.broadcasted_iota(jnp.int32, (MOE_SUB, MOE_SUB), 0)
    c = lax.broadcasted_iota(jnp.int32, (MOE_SUB, MOE_SUB), 1)
    ltri = jnp.where(c <= r, 1.0, 0.0).astype(BF16)
    nsteps = t // MOE_SUB
    full = lambda a: pl.BlockSpec(a.shape, lambda i: (0, 0))
    row = lambda cdim: pl.BlockSpec((MOE_SUB, cdim), lambda i: (i, 0))
    return pl.pallas_call(
        functools.partial(_router_kernel, subs_per_tile=moe_tile // MOE_SUB),
        grid=(nsteps,),
        in_specs=[row(D_MODEL), full(g), full(wr_hi), full(wr_lo), full(ltri)],
        out_specs=[row(D_MODEL), row(N_EXPERTS), row(N_EXPERTS),
                   pl.BlockSpec((None, 1, N_EXPERTS), lambda i: (i, 0, 0))],
        out_shape=[jax.ShapeDtypeStruct((t, D_MODEL), BF16),
                   jax.ShapeDtypeStruct((t, N_EXPERTS), F32),
                   jax.ShapeDtypeStruct((t, N_EXPERTS), jnp.int32),
                   jax.ShapeDtypeStruct((nsteps, 1, N_EXPERTS), jnp.int32)],
        scratch_shapes=[pltpu.VMEM((1, N_EXPERTS), F32)],
        compiler_params=_params("arbitrary"),
        name="norm_router",
    )(h, g, wr_hi, wr_lo, ltri)


def _moe_kernel(cum_ref, x_ref, rank_ref, gate_ref, wg_ref, wu_ref, wd_ref, o_ref,
                xc_ref, yc_ref, tmp_ref):
    i, e, f = pl.program_id(0), pl.program_id(1), pl.program_id(2)
    tile = x_ref.shape[0]
    nsub = tile // MOE_SUB
    base = (i * N_EXPERTS + e) * (nsub + 1)
    nchunk = (cum_ref[base + nsub] + MOE_CHUNK - 1) // MOE_CHUNK

    @pl.when((e == 0) & (f == 0))
    def _():
        o_ref[...] = jnp.zeros_like(o_ref)

    def overlaps(c, s):
        return (cum_ref[base + s] < (c + 1) * MOE_CHUNK) & (cum_ref[base + s + 1] > c * MOE_CHUNK)

    def match(c, s):
        rows = c * MOE_CHUNK + lax.broadcasted_iota(jnp.int32, (MOE_CHUNK, MOE_SUB), 0)
        return rank_ref[:, s * MOE_SUB:(s + 1) * MOE_SUB] == rows

    @pl.when(f == 0)
    def _():
        def gather(c, _):
            tmp_ref[...] = jnp.zeros_like(tmp_ref)
            for s in range(nsub):
                @pl.when(overlaps(c, s))
                def _():
                    p = jnp.where(match(c, s), 1.0, 0.0).astype(BF16)
                    tmp_ref[...] += jnp.dot(p, x_ref[s * MOE_SUB:(s + 1) * MOE_SUB, :],
                                            preferred_element_type=F32)
            xc_ref[pl.ds(pl.multiple_of(c * MOE_CHUNK, MOE_CHUNK), MOE_CHUNK), :] = (
                tmp_ref[...].astype(BF16))
            return 0
        lax.fori_loop(0, nchunk, gather, 0)

    def ffn(c, _):
        rows = pl.ds(pl.multiple_of(c * MOE_CHUNK, MOE_CHUNK), MOE_CHUNK)
        y = _swiglu_partial(xc_ref[rows, :], wg_ref, wu_ref, wd_ref, MOE_FF_SUB)

        @pl.when(f == 0)
        def _():
            yc_ref[rows, :] = y

        @pl.when(f != 0)
        def _():
            yc_ref[rows, :] += y
        return 0
    lax.fori_loop(0, nchunk, ffn, 0)

    @pl.when(f == pl.num_programs(2) - 1)
    def _():
        def scatter(c, _):
            y = yc_ref[pl.ds(pl.multiple_of(c * MOE_CHUNK, MOE_CHUNK), MOE_CHUNK), :].astype(BF16)
            for s in range(nsub):
                @pl.when(overlaps(c, s))
                def _():
                    gate = gate_ref[:, s * MOE_SUB:(s + 1) * MOE_SUB]
                    pg = jnp.where(match(c, s), gate, 0.0).astype(BF16)
                    upd = lax.dot_general(pg, y, (((0,), (0,)), ((), ())),
                                          preferred_element_type=F32)
                    rows = slice(s * MOE_SUB, (s + 1) * MOE_SUB)
                    o_ref[rows, :] = (o_ref[rows, :].astype(F32) + upd).astype(BF16)
            return 0
        lax.fori_loop(0, nchunk, scatter, 0)


def _moe(xn, gates, rank, cum_end, wg, wu, wd, moe_tile):
    t = xn.shape[0]
    nf = D_FF // FF_TILE
    ntile = t // moe_tile
    nsub = moe_tile // MOE_SUB
    capacity = -(-moe_tile // MOE_CHUNK) * MOE_CHUNK
    to_rows = lambda a: a.reshape(ntile, moe_tile, N_EXPERTS).transpose(0, 2, 1).reshape(
        ntile, N_EXPERTS, 1, moe_tile)
    cum = cum_end.reshape(ntile, nsub, N_EXPERTS).transpose(0, 2, 1)
    cum = jnp.concatenate([jnp.zeros((ntile, N_EXPERTS, 1), jnp.int32), cum], axis=-1).reshape(-1)
    meta = pl.BlockSpec((None, None, 1, moe_tile), lambda i, e, f, cum: (i, e, 0, 0))
    grid_spec = pltpu.PrefetchScalarGridSpec(
        num_scalar_prefetch=1,
        grid=(ntile, N_EXPERTS, nf),
        in_specs=[pl.BlockSpec((moe_tile, D_MODEL), lambda i, e, f, cum: (i, 0)),
                  meta, meta,
                  pl.BlockSpec((None, D_MODEL, FF_TILE), lambda i, e, f, cum: (e, 0, f)),
                  pl.BlockSpec((None, D_MODEL, FF_TILE), lambda i, e, f, cum: (e, 0, f)),
                  pl.BlockSpec((None, FF_TILE, D_MODEL), lambda i, e, f, cum: (e, f, 0))],
        out_specs=pl.BlockSpec((moe_tile, D_MODEL), lambda i, e, f, cum: (i, 0)),
        scratch_shapes=[pltpu.VMEM((capacity, D_MODEL), BF16),
                        pltpu.VMEM((capacity, D_MODEL), F32),
                        pltpu.VMEM((MOE_CHUNK, D_MODEL), F32)])
    return pl.pallas_call(
        _moe_kernel,
        grid_spec=grid_spec,
        out_shape=jax.ShapeDtypeStruct((t, D_MODEL), BF16),
        compiler_params=_params("parallel", "arbitrary", "arbitrary"),
        name="moe_swiglu",
    )(cum, xn, to_rows(rank), to_rows(gates), wg, wu, wd)


def _residual_add_kernel(h_ref, y_ref, o_ref):
    o_ref[...] = h_ref[...] + y_ref[...].astype(F32)


def _residual_add(h, y):
    t = h.shape[0]
    row = pl.BlockSpec((ROW_TILE, D_MODEL), lambda i: (i, 0))
    return pl.pallas_call(
        _residual_add_kernel,
        grid=(t // ROW_TILE,),
        in_specs=[row, row],
        out_specs=row,
        out_shape=jax.ShapeDtypeStruct((t, D_MODEL), F32),
        compiler_params=_params("parallel"),
        name="residual_add",
    )(h, y)


def _final_norm_kernel(h_ref, y_ref, g_ref, o_ref):
    o_ref[...] = _rms_norm_rows(h_ref[...] + y_ref[...].astype(F32), g_ref[...])


def _final_norm(h, y, g):
    t = h.shape[0]
    row = pl.BlockSpec((ROW_TILE, D_MODEL), lambda i: (i, 0))
    return pl.pallas_call(
        _final_norm_kernel,
        grid=(t // ROW_TILE,),
        in_specs=[row, row, pl.BlockSpec(g.shape, lambda i: (0, 0))],
        out_specs=row,
        out_shape=jax.ShapeDtypeStruct((t, D_MODEL), F32),
        compiler_params=_params("parallel"),
        name="final_norm",
    )(h, y, g)


def _block_diag(w):
    n = w.shape[0]
    eye = jnp.eye(n, dtype=w.dtype)
    return jnp.einsum("hij,hg->higj", w, eye).reshape(n * HEAD_DIM, n * HEAD_DIM)


def kernel(x, mix_norm_g, w_in, lru_conv_w, lru_conv_b, lru_wa, lru_ba, lru_wx, lru_bx, lru_lam,
           sc_conv_w, mix_out_g, w_out, ffn_norm_g, dense_wg, dense_wu, dense_wd,
           router_w, moe_wg, moe_wu, moe_wd, final_norm_g):
    batch, seq, _ = x.shape
    depth = w_in.shape[0]
    assert seq % ATT_BLOCK == 0 and seq % LRU_CHUNK == 0 and (batch * seq) % ROW_TILE == 0
    h = x.reshape(batch * seq, D_MODEL)
    row = lambda v: v.reshape(1, -1)
    moe_tile = min(MOE_TILE, batch * seq)
    assert (batch * seq) % moe_tile == 0 and moe_tile % MOE_SUB == 0
    pending = None
    for l in range(depth):
        lru, qkv, sc = _inproj(h, row(mix_norm_g[l]), w_in[l].astype(BF16))
        wgate = jnp.concatenate([_block_diag(lru_wa[l]), _block_diag(lru_wx[l])], axis=1).astype(BF16)
        bgate = jnp.concatenate([lru_ba[l], lru_bx[l]]).reshape(1, -1)
        y_lru, y_sc = _lru_sc(lru, sc, lru_conv_w[l], row(lru_conv_b[l]), wgate, bgate,
                              row(lru_lam[l]), sc_conv_w[l], batch, seq)
        y_sb = _sb_attention(qkv, batch, seq)
        h = _outproj(y_lru, y_sb, y_sc, row(mix_out_g[l]), w_out[l].astype(BF16), h)
        j = l // 2
        if l % 2 == 0:
            h = _ffn(h, row(ffn_norm_g[l]), dense_wg[j].astype(BF16), dense_wu[j].astype(BF16),
                     dense_wd[j].astype(BF16))
            pending = None
        else:
            xn, gates, rank, cum_end = _router(h, row(ffn_norm_g[l]), router_w[j], moe_tile)
            pending = _moe(xn, gates, rank, cum_end, moe_wg[j].astype(BF16),
                           moe_wu[j].astype(BF16), moe_wd[j].astype(BF16), moe_tile)
            if l + 1 < depth:
                h = _residual_add(h, pending)
    if pending is None:
        pending = jnp.zeros(h.shape, BF16)
    return _final_norm(h, pending, row(final_norm_g)).reshape(batch, seq, D_MODEL)
```

```python
import functools

import jax
import jax.numpy as jnp
from jax import lax
from jax.experimental import pallas as pl
from jax.experimental.pallas import tpu as pltpu

F32 = jnp.float32
BF16 = jnp.bfloat16

D_MODEL = 1024
HEAD_DIM = 64
D_LRU = 384
D_SB = 384
D_SC = 256
D_MIX = D_LRU + D_SB + D_SC
LRU_CONV = 4
SC_CONV = 3
LRU_C = 8.0
D_FF = 2816
N_EXPERTS = 8
EPS = 1e-6
F32_EXP_ZERO = -104.0

V7X_VMEM_LIMIT_BYTES = 56 * 1024 * 1024
SUBLANES = 8
LANES = 128

ROW_TILE = 512
LRU_CHUNK = 256
ATT_BLOCK = 256
FF_TILE = 1408
MOE_TILE = 2048
MOE_SUB = 512
MOE_CHUNK = 256
MOE_GROUP_ROWS = (256, 512, 576, 640, 768)
FF_SUB = 256
MOE_FF_SUB = 704


def _params(*sem):
    return pltpu.CompilerParams(dimension_semantics=sem, vmem_limit_bytes=V7X_VMEM_LIMIT_BYTES)


def _rms_norm_rows(x, g):
    ms = jnp.mean(x * x, axis=-1, keepdims=True)
    return x * lax.rsqrt(ms + EPS) * g


def _sigmoid(x):
    return 1.0 / (1.0 + jnp.exp(-x))


def _head_norm(y, g, avg):
    ms = jnp.dot((y * y).astype(BF16), avg, preferred_element_type=F32)
    return (y * lax.rsqrt(ms + EPS) * g).astype(BF16)


def _head_avg_matrix(width):
    r = lax.broadcasted_iota(jnp.int32, (width, width), 0) // HEAD_DIM
    c = lax.broadcasted_iota(jnp.int32, (width, width), 1) // HEAD_DIM
    return jnp.where(r == c, 1.0 / HEAD_DIM, 0.0).astype(BF16)


def _inproj_kernel(h_ref, g_ref, w_ref, lru_ref, qkv_ref, sc_ref):
    xn = _rms_norm_rows(h_ref[...], g_ref[...]).astype(BF16)
    c0, c1 = 2 * D_LRU, 2 * D_LRU + 3 * D_SB
    lru_ref[...] = jnp.dot(xn, w_ref[:, 0:c0], preferred_element_type=F32).astype(BF16)
    qkv_ref[...] = jnp.dot(xn, w_ref[:, c0:c1], preferred_element_type=F32).astype(BF16)
    sc_ref[...] = jnp.dot(xn, w_ref[:, c1:], preferred_element_type=F32).astype(BF16)


def _inproj(h, g, w):
    t = h.shape[0]
    d_in = w.shape[1]
    row = lambda c: pl.BlockSpec((ROW_TILE, c), lambda i: (i, 0))
    full = lambda a: pl.BlockSpec(a.shape, lambda i: (0, 0))
    return pl.pallas_call(
        _inproj_kernel,
        grid=(t // ROW_TILE,),
        in_specs=[row(D_MODEL), full(g), full(w)],
        out_specs=[row(2 * D_LRU), row(3 * D_SB), row(3 * D_SC)],
        out_shape=[jax.ShapeDtypeStruct((t, 2 * D_LRU), BF16),
                   jax.ShapeDtypeStruct((t, 3 * D_SB), BF16),
                   jax.ShapeDtypeStruct((t, d_in - 2 * D_LRU - 3 * D_SB), BF16)],
        compiler_params=_params("parallel"),
        name="norm_inproj",
    )(h, g, w)


def _lru_sc_kernel(lru_ref, sc_ref, cw_ref, cb_ref, wgate_ref, bgate_ref, lam_ref, scw_ref,
                   glru_ref, gsc_ref, avg_ref,
                   ylru_ref, ysc_ref, xbuf, pbuf, abuf, bbuf, hcar):
    tc = lru_ref.shape[0]
    pad = tc // 2

    @pl.when(pl.program_id(1) == 0)
    def _():
        xbuf[0:SUBLANES, :] = jnp.zeros((SUBLANES, D_LRU), F32)
        pbuf[0:SUBLANES, :] = jnp.zeros((SUBLANES, D_SC), F32)
        hcar[...] = jnp.zeros_like(hcar)
        abuf[0:pad, :] = jnp.ones((pad, D_LRU), F32)
        bbuf[0:pad, :] = jnp.zeros((pad, D_LRU), F32)

    xbuf[SUBLANES:SUBLANES + tc, :] = lru_ref[:, 0:D_LRU].astype(F32)
    conv = cb_ref[...]
    for k in range(LRU_CONV):
        off = SUBLANES - (LRU_CONV - 1) + k
        conv = conv + cw_ref[k:k + 1, :] * xbuf[pl.ds(off, tc), :]
    xbuf[0:SUBLANES, :] = xbuf[tc:tc + SUBLANES, :]

    gates = jnp.dot(conv.astype(BF16), wgate_ref[...], preferred_element_type=F32) + bgate_ref[...]
    gate_r = _sigmoid(gates[:, 0:D_LRU])
    gate_i = _sigmoid(gates[:, D_LRU:])
    lam = lam_ref[...]
    log_sig_lam = jnp.minimum(lam, 0.0) - jnp.log1p(jnp.exp(-jnp.abs(lam)))
    log_a = (LRU_C * gate_r) * log_sig_lam
    a = jnp.exp(log_a)
    u = jnp.sqrt(1.0 - a * a) * (gate_i * conv)

    s = 1
    while s < tc:
        abuf[pad:pad + tc, :] = a
        bbuf[pad:pad + tc, :] = u
        a_sh = abuf[pl.ds(pad - s, tc), :]
        u_sh = bbuf[pl.ds(pad - s, tc), :]
        u = a * u_sh + u
        a = a * a_sh
        s *= 2
    h = a * hcar[0:1, :] + u
    hcar[...] = jnp.broadcast_to(h[tc - 1:tc, :], hcar.shape)

    ylru_ref[...] = _head_norm(h * jax.nn.gelu(lru_ref[:, D_LRU:].astype(F32)), glru_ref[...],
                               avg_ref[...])

    sc_b = sc_ref[:, 0:D_SC].astype(F32)
    pbuf[SUBLANES:SUBLANES + tc, :] = (sc_ref[:, D_SC:2 * D_SC].astype(F32)
                                       * sc_ref[:, 2 * D_SC:].astype(F32))
    acc = jnp.zeros((tc, D_SC), F32)
    for k in range(SC_CONV):
        off = SUBLANES - (SC_CONV - 1) + k
        acc = acc + scw_ref[k:k + 1, :] * pbuf[pl.ds(off, tc), :]
    pbuf[0:SUBLANES, :] = pbuf[tc:tc + SUBLANES, :]
    ysc_ref[...] = _head_norm(sc_b * acc, gsc_ref[...], avg_ref[0:D_SC, 0:D_SC])


def _lru_sc(lru, sc, cw, cb, wgate, bgate, lam, scw, g_lru, g_sc, batch, seq):
    tc = LRU_CHUNK
    nt = seq // tc
    avg = _head_avg_matrix(D_LRU)
    row = lambda c: pl.BlockSpec((tc, c), lambda b, t: (b * nt + t, 0))
    full = lambda a: pl.BlockSpec(a.shape, lambda b, t: (0, 0))
    return pl.pallas_call(
        _lru_sc_kernel,
        grid=(batch, nt),
        in_specs=[row(2 * D_LRU), row(3 * D_SC), full(cw), full(cb), full(wgate), full(bgate),
                  full(lam), full(scw), full(g_lru), full(g_sc), full(avg)],
        out_specs=[row(D_LRU), row(D_SC)],
        out_shape=[jax.ShapeDtypeStruct((batch * seq, D_LRU), BF16),
                   jax.ShapeDtypeStruct((batch * seq, D_SC), BF16)],
        scratch_shapes=[pltpu.VMEM((tc + SUBLANES, D_LRU), F32),
                        pltpu.VMEM((tc + SUBLANES, D_SC), F32),
                        pltpu.VMEM((tc + tc // 2, D_LRU), F32),
                        pltpu.VMEM((tc + tc // 2, D_LRU), F32),
                        pltpu.VMEM((SUBLANES, D_LRU), F32)],
        compiler_params=_params("parallel", "arbitrary"),
        name="lru_shortconv",
    )(lru, sc, cw, cb, wgate, bgate, lam, scw, g_lru, g_sc, avg)


def _sb_attn_kernel(q_ref, k_ref, v_ref, cm_ref, g_ref, avg_ref, o_ref, o_acc, c_acc):
    blk = q_ref.shape[0]
    npair = q_ref.shape[1] // LANES
    i = pl.program_id(1)
    head0 = lax.broadcasted_iota(jnp.int32, (1, LANES), 1) < HEAD_DIM
    lanes = lambda p: slice(p * LANES, (p + 1) * LANES)

    def stacked_q(p):
        q = q_ref[:, lanes(p)] * jnp.asarray(HEAD_DIM ** -0.5, BF16)
        zero = jnp.zeros_like(q)
        return jnp.concatenate([jnp.where(head0, q, zero), jnp.where(head0, zero, q)], axis=0)

    def step(p, qq, j, o, carry, strict_mask=None, valid=None):
        start = pl.multiple_of(j * blk, blk)
        kt = k_ref[pl.ds(start, blk), lanes(p)]
        vt = v_ref[pl.ds(start, blk), lanes(p)]
        z = lax.dot_general(qq, kt, (((1,), (1,)), ((), ())), preferred_element_type=F32)
        neg_abs = lax.bitcast_convert_type(
            lax.bitcast_convert_type(z, jnp.uint32) | jnp.uint32(0x80000000), F32)
        soft = jnp.log(1.0 + jnp.exp(neg_abs))
        log_beta = jnp.minimum(z, 0.0) - soft
        log_keep = log_beta - z
        keep = strict_mask if valid is None else valid
        if keep is not None:
            log_keep = jnp.where(keep, log_keep, 0.0)
        cs = jnp.dot(log_keep.astype(BF16), cm_ref[...], preferred_element_type=F32)
        later = cs[:, 0:blk] + jnp.concatenate([carry] * (blk // LANES), axis=1)
        w = jnp.exp(log_beta + later)
        if keep is not None:
            w = jnp.where(keep, w, 0.0)
        o = o + jnp.dot(w.astype(BF16), vt, preferred_element_type=F32)
        return o, carry + cs[:, blk:]

    row = lax.broadcasted_iota(jnp.int32, (2 * blk, blk), 0)
    col = lax.broadcasted_iota(jnp.int32, (2 * blk, blk), 1)
    strict = col < jnp.where(row >= blk, row - blk, row)
    zeros = jnp.zeros((2 * blk, LANES), F32)
    qqs = [stacked_q(p) for p in range(npair)]
    live = None
    for p in range(npair):
        o, carry = step(p, qqs[p], i, zeros, zeros, strict_mask=strict)
        o, carry = step(p, qqs[p], jnp.maximum(i - 1, 0), o, carry, valid=i > 0)
        o_acc[p] = o
        c_acc[p] = carry
        top = jnp.max(carry)
        live = top if live is None else jnp.maximum(live, top)

    def cond(state):
        n, live = state
        return jnp.logical_and(n < i, live > F32_EXP_ZERO)

    def body(state):
        n, _ = state
        live = None
        for p in range(npair):
            o, carry = step(p, qqs[p], i - 1 - n, o_acc[p], c_acc[p])
            o_acc[p] = o
            c_acc[p] = carry
            top = jnp.max(carry)
            live = top if live is None else jnp.maximum(live, top)
        return n + 1, live

    lax.while_loop(cond, body, (jnp.int32(1), live))

    for p in range(npair):
        o = o_acc[p]
        o_ref[:, lanes(p)] = _head_norm(jnp.where(head0, o[0:blk, :], o[blk:, :]),
                                        g_ref[:, lanes(p)], avg_ref[...])


def _cumsum_matrix(blk):
    j = lax.broadcasted_iota(jnp.int32, (blk, blk + LANES), 0)
    s = lax.broadcasted_iota(jnp.int32, (blk, blk + LANES), 1)
    return jnp.where((j > s) | (s >= blk), 1.0, 0.0).astype(BF16)


def _sb_attention(qkv, g_sb, batch, seq):
    blk = ATT_BLOCK
    nq = seq // blk
    npair = D_SB // LANES
    cm = _cumsum_matrix(blk)
    avg = _head_avg_matrix(LANES)
    return pl.pallas_call(
        _sb_attn_kernel,
        grid=(batch, nq),
        in_specs=[pl.BlockSpec((blk, D_SB), lambda b, i: (b * nq + i, 0)),
                  pl.BlockSpec((seq, D_SB), lambda b, i: (b, 1)),
                  pl.BlockSpec((seq, D_SB), lambda b, i: (b, 2)),
                  pl.BlockSpec(cm.shape, lambda b, i: (0, 0)),
                  pl.BlockSpec(g_sb.shape, lambda b, i: (0, 0)),
                  pl.BlockSpec(avg.shape, lambda b, i: (0, 0))],
        out_specs=pl.BlockSpec((blk, D_SB), lambda b, i: (b * nq + i, 0)),
        out_shape=jax.ShapeDtypeStruct((batch * seq, D_SB), BF16),
        scratch_shapes=[pltpu.VMEM((npair, 2 * blk, LANES), F32),
                        pltpu.VMEM((npair, 2 * blk, LANES), F32)],
        compiler_params=_params("parallel", "arbitrary"),
        name="stickbreak_attn",
    )(qkv, qkv, qkv, cm, g_sb, avg)


def _outproj_kernel(ylru_ref, ysb_ref, ysc_ref, w_ref, h_ref, o_ref, ycat_ref):
    lo = 0
    for y_ref in (ylru_ref, ysb_ref, ysc_ref):
        c = y_ref.shape[1]
        ycat_ref[:, lo:lo + c] = y_ref[...]
        lo += c
    o_ref[...] = h_ref[...] + jnp.dot(ycat_ref[...], w_ref[...], preferred_element_type=F32)


def _outproj(ylru, ysb, ysc, w, h):
    t = h.shape[0]
    row = lambda cdim: pl.BlockSpec((ROW_TILE, cdim), lambda i: (i, 0))
    return pl.pallas_call(
        _outproj_kernel,
        grid=(t // ROW_TILE,),
        in_specs=[row(D_LRU), row(D_SB), row(D_SC), pl.BlockSpec(w.shape, lambda i: (0, 0)),
                  row(D_MODEL)],
        out_specs=row(D_MODEL),
        out_shape=jax.ShapeDtypeStruct((t, D_MODEL), F32),
        scratch_shapes=[pltpu.VMEM((ROW_TILE, D_MIX), BF16)],
        compiler_params=_params("parallel"),
        name="outproj",
    )(ylru, ysb, ysc, w, h)


def _swiglu_partial(xn, wg_ref, wu_ref, wd_ref, sub):
    width = wg_ref.shape[1]
    out = None
    for lo in range(0, width, sub):
        hi = min(lo + sub, width)
        a = jnp.dot(xn, wg_ref[:, lo:hi], preferred_element_type=F32)
        b = jnp.dot(xn, wu_ref[:, lo:hi], preferred_element_type=F32)
        hid = (a * _sigmoid(a) * b).astype(BF16)
        part = jnp.dot(hid, wd_ref[lo:hi, :], preferred_element_type=F32)
        out = part if out is None else out + part
    return out


def _ffn_kernel(h_ref, g_ref, wg_ref, wu_ref, wd_ref, o_ref):
    h = h_ref[...]
    xn = _rms_norm_rows(h, g_ref[...]).astype(BF16)
    o_ref[...] = h + _swiglu_partial(xn, wg_ref, wu_ref, wd_ref, FF_SUB)


def _ffn(h, g, wg, wu, wd):
    t = h.shape[0]
    row = pl.BlockSpec((ROW_TILE, D_MODEL), lambda i: (i, 0))
    full = lambda a: pl.BlockSpec(a.shape, lambda i: (0, 0))
    return pl.pallas_call(
        _ffn_kernel,
        grid=(t // ROW_TILE,),
        in_specs=[row, full(g), full(wg), full(wu), full(wd)],
        out_specs=row,
        out_shape=jax.ShapeDtypeStruct((t, D_MODEL), F32),
        compiler_params=_params("parallel"),
        name="norm_swiglu",
    )(h, g, wg, wu, wd)


def _top2_gates(logits):
    lane = lax.broadcasted_iota(jnp.int32, logits.shape, 1)
    m1 = jnp.max(logits, axis=-1, keepdims=True)
    i1 = jnp.min(jnp.where(logits == m1, lane, N_EXPERTS), axis=-1, keepdims=True)
    first = lane == i1
    rest = jnp.where(first, -jnp.inf, logits)
    m2 = jnp.max(rest, axis=-1, keepdims=True)
    i2 = jnp.min(jnp.where(rest == m2, lane, N_EXPERTS), axis=-1, keepdims=True)
    e2 = jnp.exp(m2 - m1)
    w1 = 1.0 / (1.0 + e2)
    return jnp.where(first, w1, jnp.where(lane == i2, e2 * w1, 0.0))


def _router_logits(xn_f32, wr_hi, wr_lo):
    x_hi = xn_f32.astype(BF16)
    x_lo = (xn_f32 - x_hi.astype(F32)).astype(BF16)
    return (jnp.dot(x_hi, wr_hi, preferred_element_type=F32)
            + jnp.dot(x_hi, wr_lo, preferred_element_type=F32)
            + jnp.dot(x_lo, wr_hi, preferred_element_type=F32))


def _router_kernel(h_ref, g_ref, wrh_ref, wrl_ref, ltri_ref, xn_ref, gate_ref, rank_ref, cum_ref,
                   base_ref, *, subs_per_tile):
    @pl.when(pl.program_id(0) % subs_per_tile == 0)
    def _():
        base_ref[...] = jnp.zeros_like(base_ref)

    xn = _rms_norm_rows(h_ref[...], g_ref[...])
    xn_ref[...] = xn.astype(BF16)
    gates = _top2_gates(_router_logits(xn, wrh_ref[...], wrl_ref[...]))
    gate_ref[...] = gates
    sel = gates > 0.0
    inc = jnp.dot(ltri_ref[...], jnp.where(sel, 1.0, 0.0).astype(BF16), preferred_element_type=F32)
    count = base_ref[...] + inc
    rank_ref[...] = jnp.where(sel, count - 1.0, -1.0).astype(jnp.int32)
    total = count[MOE_SUB - 1:MOE_SUB, :]
    base_ref[...] = total
    cum_ref[...] = total.astype(jnp.int32)


def _router(h, g, wr, moe_tile):
    t = h.shape[0]
    wr_hi = wr.astype(BF16)
    wr_lo = (wr - wr_hi.astype(F32)).astype(BF16)
    r = lax.broadcasted_iota(jnp.int32, (MOE_SUB, MOE_SUB), 0)
    c = lax.broadcasted_iota(jnp.int32, (MOE_SUB, MOE_SUB), 1)
    ltri = jnp.where(c <= r, 1.0, 0.0).astype(BF16)
    nsteps = t // MOE_SUB
    full = lambda a: pl.BlockSpec(a.shape, lambda i: (0, 0))
    row = lambda cdim: pl.BlockSpec((MOE_SUB, cdim), lambda i: (i, 0))
    return pl.pallas_call(
        functools.partial(_router_kernel, subs_per_tile=moe_tile // MOE_SUB),
        grid=(nsteps,),
        in_specs=[row(D_MODEL), full(g), full(wr_hi), full(wr_lo), full(ltri)],
        out_specs=[row(D_MODEL), row(N_EXPERTS), row(N_EXPERTS),
                   pl.BlockSpec((None, 1, N_EXPERTS), lambda i: (i, 0, 0))],
        out_shape=[jax.ShapeDtypeStruct((t, D_MODEL), BF16),
                   jax.ShapeDtypeStruct((t, N_EXPERTS), F32),
                   jax.ShapeDtypeStruct((t, N_EXPERTS), jnp.int32),
                   jax.ShapeDtypeStruct((nsteps, 1, N_EXPERTS), jnp.int32)],
        scratch_shapes=[pltpu.VMEM((1, N_EXPERTS), F32)],
        compiler_params=_params("arbitrary"),
        name="norm_router",
    )(h, g, wr_hi, wr_lo, ltri)


def _moe_kernel(cum_ref, x_ref, rank_ref, gate_ref, wg_ref, wu_ref, wd_ref, o_ref,
                xc_ref, yc_ref, tmp_ref):
    i, e, f = pl.program_id(0), pl.program_id(1), pl.program_id(2)
    tile = x_ref.shape[0]
    nsub = tile // MOE_SUB
    base = (i * N_EXPERTS + e) * (nsub + 1)
    nchunk = (cum_ref[base + nsub] + MOE_CHUNK - 1) // MOE_CHUNK

    @pl.when((e == 0) & (f == 0))
    def _():
        o_ref[...] = jnp.zeros_like(o_ref)

    def overlaps(c, s):
        return (cum_ref[base + s] < (c + 1) * MOE_CHUNK) & (cum_ref[base + s + 1] > c * MOE_CHUNK)

    def match(c, s):
        rows = c * MOE_CHUNK + lax.broadcasted_iota(jnp.int32, (MOE_CHUNK, MOE_SUB), 0)
        return rank_ref[:, s * MOE_SUB:(s + 1) * MOE_SUB] == rows

    @pl.when(f == 0)
    def _():
        def gather(c, _):
            tmp_ref[...] = jnp.zeros_like(tmp_ref)
            for s in range(nsub):
                @pl.when(overlaps(c, s))
                def _():
                    p = jnp.where(match(c, s), 1.0, 0.0).astype(BF16)
                    tmp_ref[...] += jnp.dot(p, x_ref[s * MOE_SUB:(s + 1) * MOE_SUB, :],
                                            preferred_element_type=F32)
            xc_ref[pl.ds(pl.multiple_of(c * MOE_CHUNK, MOE_CHUNK), MOE_CHUNK), :] = (
                tmp_ref[...].astype(BF16))
            return 0
        lax.fori_loop(0, nchunk, gather, 0)

    def ffn_rows(start, m):
        rows = pl.ds(start, m)
        y = _swiglu_partial(xc_ref[rows, :], wg_ref, wu_ref, wd_ref, MOE_FF_SUB)
        tail = -m % MOE_CHUNK

        @pl.when(f == 0)
        def _():
            yc_ref[rows, :] = y
            if tail:
                yc_ref[pl.ds(start + m, tail), :] = jnp.zeros((tail, D_MODEL), F32)

        @pl.when(f != 0)
        def _():
            yc_ref[rows, :] += y

    count = cum_ref[base + nsub]
    lo = 0
    for m in [v for v in MOE_GROUP_ROWS if v <= xc_ref.shape[0]]:
        @pl.when((count > lo) & (count <= m))
        def _():
            ffn_rows(0, m)
        lo = m

    @pl.when(count > lo)
    def _():
        ffn_rows(0, lo)

        def extra(c, _):
            ffn_rows(pl.multiple_of(c * MOE_CHUNK, MOE_CHUNK), MOE_CHUNK)
            return 0
        lax.fori_loop(lo // MOE_CHUNK, nchunk, extra, 0)

    @pl.when(f == pl.num_programs(2) - 1)
    def _():
        def scatter(c, _):
            y = yc_ref[pl.ds(pl.multiple_of(c * MOE_CHUNK, MOE_CHUNK), MOE_CHUNK), :].astype(BF16)
            for s in range(nsub):
                @pl.when(overlaps(c, s))
                def _():
                    gate = gate_ref[:, s * MOE_SUB:(s + 1) * MOE_SUB]
                    pg = jnp.where(match(c, s), gate, 0.0).astype(BF16)
                    upd = lax.dot_general(pg, y, (((0,), (0,)), ((), ())),
                                          preferred_element_type=F32)
                    rows = slice(s * MOE_SUB, (s + 1) * MOE_SUB)
                    o_ref[rows, :] = (o_ref[rows, :].astype(F32) + upd).astype(BF16)
            return 0
        lax.fori_loop(0, nchunk, scatter, 0)


def _moe(xn, gates, rank, cum_end, wg, wu, wd, moe_tile):
    t = xn.shape[0]
    nf = D_FF // FF_TILE
    ntile = t // moe_tile
    nsub = moe_tile // MOE_SUB
    capacity = -(-moe_tile // MOE_CHUNK) * MOE_CHUNK
    to_rows = lambda a: a.reshape(ntile, moe_tile, N_EXPERTS).transpose(0, 2, 1).reshape(
        ntile, N_EXPERTS, 1, moe_tile)
    cum = cum_end.reshape(ntile, nsub, N_EXPERTS).transpose(0, 2, 1)
    cum = jnp.concatenate([jnp.zeros((ntile, N_EXPERTS, 1), jnp.int32), cum], axis=-1).reshape(-1)
    meta = pl.BlockSpec((None, None, 1, moe_tile), lambda i, e, f, cum: (i, e, 0, 0))
    grid_spec = pltpu.PrefetchScalarGridSpec(
        num_scalar_prefetch=1,
        grid=(ntile, N_EXPERTS, nf),
        in_specs=[pl.BlockSpec((moe_tile, D_MODEL), lambda i, e, f, cum: (i, 0)),
                  meta, meta,
                  pl.BlockSpec((None, D_MODEL, FF_TILE), lambda i, e, f, cum: (e, 0, f)),
                  pl.BlockSpec((None, D_MODEL, FF_TILE), lambda i, e, f, cum: (e, 0, f)),
                  pl.BlockSpec((None, FF_TILE, D_MODEL), lambda i, e, f, cum: (e, f, 0))],
        out_specs=pl.BlockSpec((moe_tile, D_MODEL), lambda i, e, f, cum: (i, 0)),
        scratch_shapes=[pltpu.VMEM((capacity, D_MODEL), BF16),
                        pltpu.VMEM((capacity, D_MODEL), F32),
                        pltpu.VMEM((MOE_CHUNK, D_MODEL), F32)])
    return pl.pallas_call(
        _moe_kernel,
        grid_spec=grid_spec,
        out_shape=jax.ShapeDtypeStruct((t, D_MODEL), BF16),
        compiler_params=_params("parallel", "arbitrary", "arbitrary"),
        name="moe_swiglu",
    )(cum, xn, to_rows(rank), to_rows(gates), wg, wu, wd)


def _residual_add_kernel(h_ref, y_ref, o_ref):
    o_ref[...] = h_ref[...] + y_ref[...].astype(F32)


def _residual_add(h, y):
    t = h.shape[0]
    row = pl.BlockSpec((ROW_TILE, D_MODEL), lambda i: (i, 0))
    return pl.pallas_call(
        _residual_add_kernel,
        grid=(t // ROW_TILE,),
        in_specs=[row, row],
        out_specs=row,
        out_shape=jax.ShapeDtypeStruct((t, D_MODEL), F32),
        compiler_params=_params("parallel"),
        name="residual_add",
    )(h, y)


def _final_norm_kernel(h_ref, y_ref, g_ref, o_ref):
    o_ref[...] = _rms_norm_rows(h_ref[...] + y_ref[...].astype(F32), g_ref[...])


def _final_norm(h, y, g):
    t = h.shape[0]
    row = pl.BlockSpec((ROW_TILE, D_MODEL), lambda i: (i, 0))
    return pl.pallas_call(
        _final_norm_kernel,
        grid=(t // ROW_TILE,),
        in_specs=[row, row, pl.BlockSpec(g.shape, lambda i: (0, 0))],
        out_specs=row,
        out_shape=jax.ShapeDtypeStruct((t, D_MODEL), F32),
        compiler_params=_params("parallel"),
        name="final_norm",
    )(h, y, g)


def _block_diag(w):
    n = w.shape[0]
    eye = jnp.eye(n, dtype=w.dtype)
    return jnp.einsum("hij,hg->higj", w, eye).reshape(n * HEAD_DIM, n * HEAD_DIM)


def kernel(x, mix_norm_g, w_in, lru_conv_w, lru_conv_b, lru_wa, lru_ba, lru_wx, lru_bx, lru_lam,
           sc_conv_w, mix_out_g, w_out, ffn_norm_g, dense_wg, dense_wu, dense_wd,
           router_w, moe_wg, moe_wu, moe_wd, final_norm_g):
    batch, seq, _ = x.shape
    depth = w_in.shape[0]
    assert seq % ATT_BLOCK == 0 and seq % LRU_CHUNK == 0 and (batch * seq) % ROW_TILE == 0
    h = x.reshape(batch * seq, D_MODEL)
    row = lambda v: v.reshape(1, -1)
    moe_tile = min(MOE_TILE, batch * seq)
    assert (batch * seq) % moe_tile == 0 and moe_tile % MOE_SUB == 0
    pending = None
    for l in range(depth):
        lru, qkv, sc = _inproj(h, row(mix_norm_g[l]), w_in[l].astype(BF16))
        wgate = jnp.concatenate([_block_diag(lru_wa[l]), _block_diag(lru_wx[l])], axis=1).astype(BF16)
        bgate = jnp.concatenate([lru_ba[l], lru_bx[l]]).reshape(1, -1)
        g_mix = row(mix_out_g[l])
        y_lru, y_sc = _lru_sc(lru, sc, lru_conv_w[l], row(lru_conv_b[l]), wgate, bgate,
                              row(lru_lam[l]), sc_conv_w[l], g_mix[:, 0:D_LRU],
                              g_mix[:, D_LRU + D_SB:], batch, seq)
        y_sb = _sb_attention(qkv, g_mix[:, D_LRU:D_LRU + D_SB], batch, seq)
        h = _outproj(y_lru, y_sb, y_sc, w_out[l].astype(BF16), h)
        j = l // 2
        if l % 2 == 0:
            h = _ffn(h, row(ffn_norm_g[l]), dense_wg[j].astype(BF16), dense_wu[j].astype(BF16),
                     dense_wd[j].astype(BF16))
            pending = None
        else:
            xn, gates, rank, cum_end = _router(h, row(ffn_norm_g[l]), router_w[j], moe_tile)
            pending = _moe(xn, gates, rank, cum_end, moe_wg[j].astype(BF16),
                           moe_wu[j].astype(BF16), moe_wd[j].astype(BF16), moe_tile)
            if l + 1 < depth:
                h = _residual_add(h, pending)
    if pending is None:
        pending = jnp.zeros(h.shape, BF16)
    return _final_norm(h, pending, row(final_norm_g)).reshape(batch, seq, D_MODEL)
```

```python
import functools

import jax
import jax.numpy as jnp
from jax import lax
from jax.experimental import pallas as pl
from jax.experimental.pallas import tpu as pltpu

F32 = jnp.float32
BF16 = jnp.bfloat16

D_MODEL = 1024
HEAD_DIM = 64
D_LRU = 384
D_SB = 384
D_SC = 256
D_MIX = D_LRU + D_SB + D_SC
LRU_CONV = 4
SC_CONV = 3
LRU_C = 8.0
D_FF = 2816
N_EXPERTS = 8
EPS = 1e-6
F32_EXP_ZERO = -104.0

V7X_VMEM_LIMIT_BYTES = 56 * 1024 * 1024
SUBLANES = 8
LANES = 128

ROW_TILE = 512
LRU_CHUNK = 256
ATT_BLOCK = 256
FF_TILE = 1408
MOE_TILE = 2048
MOE_SUB = 512
ROUTER_TILE = 1024
MOE_CHUNK = 256
MOE_GROUP_ROWS = (256, 512, 576, 640, 768)
FF_SUB = 256
MOE_FF_SUB = 704


def _params(*sem):
    return pltpu.CompilerParams(dimension_semantics=sem, vmem_limit_bytes=V7X_VMEM_LIMIT_BYTES)


def _rms_norm_rows(x, g):
    ms = jnp.mean(x * x, axis=-1, keepdims=True)
    return x * lax.rsqrt(ms + EPS) * g


def _sigmoid(x):
    return 1.0 / (1.0 + jnp.exp(-x))


def _head_norm(y, g, avg):
    ms = jnp.dot((y * y).astype(BF16), avg, preferred_element_type=F32)
    return (y * lax.rsqrt(ms + EPS) * g).astype(BF16)


def _head_avg_matrix(width):
    r = lax.broadcasted_iota(jnp.int32, (width, width), 0) // HEAD_DIM
    c = lax.broadcasted_iota(jnp.int32, (width, width), 1) // HEAD_DIM
    return jnp.where(r == c, 1.0 / HEAD_DIM, 0.0).astype(BF16)


def _inproj_kernel(*refs, has_residual):
    if has_residual:
        h_ref, y_ref, g_ref, w_ref, hout_ref, lru_ref, qkv_ref, sc_ref = refs
        h = h_ref[...] + y_ref[...].astype(F32)
        hout_ref[...] = h
    else:
        h_ref, g_ref, w_ref, lru_ref, qkv_ref, sc_ref = refs
        h = h_ref[...]
    xn = _rms_norm_rows(h, g_ref[...]).astype(BF16)
    c0, c1 = 2 * D_LRU, 2 * D_LRU + 3 * D_SB
    lru_ref[...] = jnp.dot(xn, w_ref[:, 0:c0], preferred_element_type=F32).astype(BF16)
    qkv_ref[...] = jnp.dot(xn, w_ref[:, c0:c1], preferred_element_type=F32).astype(BF16)
    sc_ref[...] = jnp.dot(xn, w_ref[:, c1:], preferred_element_type=F32).astype(BF16)


def _inproj(h, y, g, w_all, layer):
    t = h.shape[0]
    d_in = w_all.shape[2]
    row = lambda c: pl.BlockSpec((ROW_TILE, c), lambda i: (i, 0))
    acts = [h] if y is None else [h, y]
    proj_shapes = [jax.ShapeDtypeStruct((t, 2 * D_LRU), BF16),
                   jax.ShapeDtypeStruct((t, 3 * D_SB), BF16),
                   jax.ShapeDtypeStruct((t, d_in - 2 * D_LRU - 3 * D_SB), BF16)]
    proj_specs = [row(2 * D_LRU), row(3 * D_SB), row(3 * D_SC)]
    out = pl.pallas_call(
        functools.partial(_inproj_kernel, has_residual=y is not None),
        grid=(t // ROW_TILE,),
        in_specs=[row(D_MODEL)] * len(acts) + [
            pl.BlockSpec(g.shape, lambda i: (0, 0)),
            pl.BlockSpec((None, D_MODEL, d_in), lambda i: (layer, 0, 0))],
        out_specs=([] if y is None else [row(D_MODEL)]) + proj_specs,
        out_shape=([] if y is None else [jax.ShapeDtypeStruct((t, D_MODEL), F32)]) + proj_shapes,
        compiler_params=_params("parallel"),
        name="norm_inproj",
    )(*acts, g, w_all)
    return (h, *out) if y is None else tuple(out)


def _lru_sc_kernel(lru_ref, sc_ref, cw_ref, cb_ref, wgate_ref, bgate_ref, lam_ref, scw_ref,
                   glru_ref, gsc_ref, avg_ref,
                   ylru_ref, ysc_ref, xbuf, pbuf, abuf, bbuf, hcar):
    tc = lru_ref.shape[0]
    pad = tc // 2

    @pl.when(pl.program_id(1) == 0)
    def _():
        xbuf[0:SUBLANES, :] = jnp.zeros((SUBLANES, D_LRU), F32)
        pbuf[0:SUBLANES, :] = jnp.zeros((SUBLANES, D_SC), F32)
        hcar[...] = jnp.zeros_like(hcar)
        abuf[0:pad, :] = jnp.ones((pad, D_LRU), F32)
        bbuf[0:pad, :] = jnp.zeros((pad, D_LRU), F32)

    xbuf[SUBLANES:SUBLANES + tc, :] = lru_ref[:, 0:D_LRU].astype(F32)
    conv = cb_ref[...]
    for k in range(LRU_CONV):
        off = SUBLANES - (LRU_CONV - 1) + k
        conv = conv + cw_ref[k:k + 1, :] * xbuf[pl.ds(off, tc), :]
    xbuf[0:SUBLANES, :] = xbuf[tc:tc + SUBLANES, :]

    gates = jnp.dot(conv.astype(BF16), wgate_ref[...], preferred_element_type=F32) + bgate_ref[...]
    gate_r = _sigmoid(gates[:, 0:D_LRU])
    gate_i = _sigmoid(gates[:, D_LRU:])
    lam = lam_ref[...]
    log_sig_lam = jnp.minimum(lam, 0.0) - jnp.log1p(jnp.exp(-jnp.abs(lam)))
    log_a = (LRU_C * gate_r) * log_sig_lam
    a = jnp.exp(log_a)
    u = jnp.sqrt(1.0 - a * a) * (gate_i * conv)

    s = 1
    while s < tc:
        abuf[pad:pad + tc, :] = a
        bbuf[pad:pad + tc, :] = u
        a_sh = abuf[pl.ds(pad - s, tc), :]
        u_sh = bbuf[pl.ds(pad - s, tc), :]
        u = a * u_sh + u
        a = a * a_sh
        s *= 2
    h = a * hcar[0:1, :] + u
    hcar[...] = jnp.broadcast_to(h[tc - 1:tc, :], hcar.shape)

    ylru_ref[...] = _head_norm(h * jax.nn.gelu(lru_ref[:, D_LRU:].astype(F32)), glru_ref[...],
                               avg_ref[...])

    sc_b = sc_ref[:, 0:D_SC].astype(F32)
    pbuf[SUBLANES:SUBLANES + tc, :] = (sc_ref[:, D_SC:2 * D_SC].astype(F32)
                                       * sc_ref[:, 2 * D_SC:].astype(F32))
    acc = jnp.zeros((tc, D_SC), F32)
    for k in range(SC_CONV):
        off = SUBLANES - (SC_CONV - 1) + k
        acc = acc + scw_ref[k:k + 1, :] * pbuf[pl.ds(off, tc), :]
    pbuf[0:SUBLANES, :] = pbuf[tc:tc + SUBLANES, :]
    ysc_ref[...] = _head_norm(sc_b * acc, gsc_ref[...], avg_ref[0:D_SC, 0:D_SC])


def _lru_sc(lru, sc, cw, cb, wgate, bgate, lam, scw, g_lru, g_sc, batch, seq):
    tc = LRU_CHUNK
    nt = seq // tc
    avg = _head_avg_matrix(D_LRU)
    row = lambda c: pl.BlockSpec((tc, c), lambda b, t: (b * nt + t, 0))
    full = lambda a: pl.BlockSpec(a.shape, lambda b, t: (0, 0))
    return pl.pallas_call(
        _lru_sc_kernel,
        grid=(batch, nt),
        in_specs=[row(2 * D_LRU), row(3 * D_SC), full(cw), full(cb), full(wgate), full(bgate),
                  full(lam), full(scw), full(g_lru), full(g_sc), full(avg)],
        out_specs=[row(D_LRU), row(D_SC)],
        out_shape=[jax.ShapeDtypeStruct((batch * seq, D_LRU), BF16),
                   jax.ShapeDtypeStruct((batch * seq, D_SC), BF16)],
        scratch_shapes=[pltpu.VMEM((tc + SUBLANES, D_LRU), F32),
                        pltpu.VMEM((tc + SUBLANES, D_SC), F32),
                        pltpu.VMEM((tc + tc // 2, D_LRU), F32),
                        pltpu.VMEM((tc + tc // 2, D_LRU), F32),
                        pltpu.VMEM((SUBLANES, D_LRU), F32)],
        compiler_params=_params("parallel", "arbitrary"),
        name="lru_shortconv",
    )(lru, sc, cw, cb, wgate, bgate, lam, scw, g_lru, g_sc, avg)


def _sb_attn_kernel(q_ref, k_ref, v_ref, cm_ref, g_ref, avg_ref, o_ref, o_acc, c_acc):
    blk = q_ref.shape[0]
    npair = q_ref.shape[1] // LANES
    i = pl.program_id(1)
    head0 = lax.broadcasted_iota(jnp.int32, (1, LANES), 1) < HEAD_DIM
    lanes = lambda p: slice(p * LANES, (p + 1) * LANES)

    def stacked_q(p):
        q = q_ref[:, lanes(p)] * jnp.asarray(HEAD_DIM ** -0.5, BF16)
        zero = jnp.zeros_like(q)
        return jnp.concatenate([jnp.where(head0, q, zero), jnp.where(head0, zero, q)], axis=0)

    def step(p, qq, j, o, carry, strict_mask=None, valid=None):
        start = pl.multiple_of(j * blk, blk)
        kt = k_ref[pl.ds(start, blk), lanes(p)]
        vt = v_ref[pl.ds(start, blk), lanes(p)]
        z = lax.dot_general(qq, kt, (((1,), (1,)), ((), ())), preferred_element_type=F32)
        neg_abs = lax.bitcast_convert_type(
            lax.bitcast_convert_type(z, jnp.uint32) | jnp.uint32(0x80000000), F32)
        soft = jnp.log(1.0 + jnp.exp(neg_abs))
        log_beta = jnp.minimum(z, 0.0) - soft
        log_keep = log_beta - z
        keep = strict_mask if valid is None else valid
        if keep is not None:
            log_keep = jnp.where(keep, log_keep, 0.0)
        cs = jnp.dot(log_keep.astype(BF16), cm_ref[...], preferred_element_type=F32)
        later = cs[:, 0:blk] + jnp.concatenate([carry] * (blk // LANES), axis=1)
        w = jnp.exp(log_beta + later)
        if keep is not None:
            w = jnp.where(keep, w, 0.0)
        o = o + jnp.dot(w.astype(BF16), vt, preferred_element_type=F32)
        return o, carry + cs[:, blk:]

    row = lax.broadcasted_iota(jnp.int32, (2 * blk, blk), 0)
    col = lax.broadcasted_iota(jnp.int32, (2 * blk, blk), 1)
    strict = col < jnp.where(row >= blk, row - blk, row)
    zeros = jnp.zeros((2 * blk, LANES), F32)
    qqs = [stacked_q(p) for p in range(npair)]
    live = None
    for p in range(npair):
        o, carry = step(p, qqs[p], i, zeros, zeros, strict_mask=strict)
        o, carry = step(p, qqs[p], jnp.maximum(i - 1, 0), o, carry, valid=i > 0)
        o_acc[p] = o
        c_acc[p] = carry
        top = jnp.max(carry)
        live = top if live is None else jnp.maximum(live, top)

    def cond(state):
        n, live = state
        return jnp.logical_and(n < i, live > F32_EXP_ZERO)

    def body(state):
        n, _ = state
        live = None
        for p in range(npair):
            o, carry = step(p, qqs[p], i - 1 - n, o_acc[p], c_acc[p])
            o_acc[p] = o
            c_acc[p] = carry
            top = jnp.max(carry)
            live = top if live is None else jnp.maximum(live, top)
        return n + 1, live

    lax.while_loop(cond, body, (jnp.int32(1), live))

    for p in range(npair):
        o = o_acc[p]
        o_ref[:, lanes(p)] = _head_norm(jnp.where(head0, o[0:blk, :], o[blk:, :]),
                                        g_ref[:, lanes(p)], avg_ref[...])


def _cumsum_matrix(blk):
    j = lax.broadcasted_iota(jnp.int32, (blk, blk + LANES), 0)
    s = lax.broadcasted_iota(jnp.int32, (blk, blk + LANES), 1)
    return jnp.where((j > s) | (s >= blk), 1.0, 0.0).astype(BF16)


def _sb_attention(qkv, g_sb, batch, seq):
    blk = ATT_BLOCK
    nq = seq // blk
    npair = D_SB // LANES
    cm = _cumsum_matrix(blk)
    avg = _head_avg_matrix(LANES)
    return pl.pallas_call(
        _sb_attn_kernel,
        grid=(batch, nq),
        in_specs=[pl.BlockSpec((blk, D_SB), lambda b, i: (b * nq + i, 0)),
                  pl.BlockSpec((seq, D_SB), lambda b, i: (b, 1)),
                  pl.BlockSpec((seq, D_SB), lambda b, i: (b, 2)),
                  pl.BlockSpec(cm.shape, lambda b, i: (0, 0)),
                  pl.BlockSpec(g_sb.shape, lambda b, i: (0, 0)),
                  pl.BlockSpec(avg.shape, lambda b, i: (0, 0))],
        out_specs=pl.BlockSpec((blk, D_SB), lambda b, i: (b * nq + i, 0)),
        out_shape=jax.ShapeDtypeStruct((batch * seq, D_SB), BF16),
        scratch_shapes=[pltpu.VMEM((npair, 2 * blk, LANES), F32),
                        pltpu.VMEM((npair, 2 * blk, LANES), F32)],
        compiler_params=_params("parallel", "arbitrary"),
        name="stickbreak_attn",
    )(qkv, qkv, qkv, cm, g_sb, avg)


def _mix_residual(ylru_ref, ysb_ref, ysc_ref, w_ref, h_ref, ycat_ref):
    lo = 0
    for y_ref in (ylru_ref, ysb_ref, ysc_ref):
        c = y_ref.shape[1]
        ycat_ref[:, lo:lo + c] = y_ref[...]
        lo += c
    return h_ref[...] + jnp.dot(ycat_ref[...], w_ref[...], preferred_element_type=F32)


def _mix_specs(w_out_all, layer, rows):
    row = lambda cdim: pl.BlockSpec((rows, cdim), lambda i: (i, 0))
    return [row(D_LRU), row(D_SB), row(D_SC),
            pl.BlockSpec((None,) + w_out_all.shape[1:], lambda i: (layer, 0, 0)), row(D_MODEL)]


def _swiglu_partial(xn, wg_ref, wu_ref, wd_ref, sub):
    width = wg_ref.shape[1]
    out = None
    for lo in range(0, width, sub):
        hi = min(lo + sub, width)
        a = jnp.dot(xn, wg_ref[:, lo:hi], preferred_element_type=F32)
        b = jnp.dot(xn, wu_ref[:, lo:hi], preferred_element_type=F32)
        hid = (a * _sigmoid(a) * b).astype(BF16)
        part = jnp.dot(hid, wd_ref[lo:hi, :], preferred_element_type=F32)
        out = part if out is None else out + part
    return out


def _ffn_kernel(ylru_ref, ysb_ref, ysc_ref, wout_ref, h_ref, g_ref, wg_ref, wu_ref, wd_ref, o_ref,
                ycat_ref):
    h = _mix_residual(ylru_ref, ysb_ref, ysc_ref, wout_ref, h_ref, ycat_ref)
    xn = _rms_norm_rows(h, g_ref[...]).astype(BF16)
    o_ref[...] = h + _swiglu_partial(xn, wg_ref, wu_ref, wd_ref, FF_SUB)


def _outproj_ffn(ys, w_out, mix_layer, h, g, wg, wu, wd, layer):
    t = h.shape[0]
    full = lambda a: pl.BlockSpec(a.shape, lambda i: (0, 0))
    stacked = lambda a: pl.BlockSpec((None,) + a.shape[1:], lambda i: (layer, 0, 0))
    return pl.pallas_call(
        _ffn_kernel,
        grid=(t // ROW_TILE,),
        in_specs=_mix_specs(w_out, mix_layer, ROW_TILE) + [full(g), stacked(wg), stacked(wu),
                                                            stacked(wd)],
        out_specs=pl.BlockSpec((ROW_TILE, D_MODEL), lambda i: (i, 0)),
        out_shape=jax.ShapeDtypeStruct((t, D_MODEL), F32),
        scratch_shapes=[pltpu.VMEM((ROW_TILE, D_MIX), BF16)],
        compiler_params=_params("parallel"),
        name="outproj_norm_swiglu",
    )(*ys, w_out, h, g, wg, wu, wd)


def _top2_gates(logits):
    lane = lax.broadcasted_iota(jnp.int32, logits.shape, 1)
    m1 = jnp.max(logits, axis=-1, keepdims=True)
    i1 = jnp.min(jnp.where(logits == m1, lane, N_EXPERTS), axis=-1, keepdims=True)
    first = lane == i1
    rest = jnp.where(first, -jnp.inf, logits)
    m2 = jnp.max(rest, axis=-1, keepdims=True)
    i2 = jnp.min(jnp.where(rest == m2, lane, N_EXPERTS), axis=-1, keepdims=True)
    e2 = jnp.exp(m2 - m1)
    w1 = 1.0 / (1.0 + e2)
    return jnp.where(first, w1, jnp.where(lane == i2, e2 * w1, 0.0))


def _router_logits(xn_f32, wr_hi, wr_lo):
    x_hi = xn_f32.astype(BF16)
    x_lo = (xn_f32 - x_hi.astype(F32)).astype(BF16)
    return (jnp.dot(x_hi, wr_hi, preferred_element_type=F32)
            + jnp.dot(x_hi, wr_lo, preferred_element_type=F32)
            + jnp.dot(x_lo, wr_hi, preferred_element_type=F32))


def _router_kernel(ylru_ref, ysb_ref, ysc_ref, wout_ref, h_ref, g_ref, wrh_ref, wrl_ref, ltri_ref,
                   hmid_ref, xn_ref, gate_ref, rank_ref, cum_ref, base_ref, ycat_ref, *,
                   subs_per_tile):
    @pl.when(pl.program_id(0) % subs_per_tile == 0)
    def _():
        base_ref[...] = jnp.zeros_like(base_ref)

    h = _mix_residual(ylru_ref, ysb_ref, ysc_ref, wout_ref, h_ref, ycat_ref)
    hmid_ref[...] = h
    xn = _rms_norm_rows(h, g_ref[...])
    xn_ref[...] = xn.astype(BF16)
    gates = _top2_gates(_router_logits(xn, wrh_ref[...], wrl_ref[...]))
    gate_ref[...] = gates
    sel = gates > 0.0
    inc = jnp.dot(ltri_ref[...], jnp.where(sel, 1.0, 0.0).astype(BF16), preferred_element_type=F32)
    count = base_ref[...] + inc
    rank_ref[...] = jnp.where(sel, count - 1.0, -1.0).astype(jnp.int32)
    rows = count.shape[0]
    for s in range(rows // MOE_SUB):
        cum_ref[s] = count[(s + 1) * MOE_SUB - 1:(s + 1) * MOE_SUB, :].astype(jnp.int32)
    base_ref[...] = count[rows - 1:rows, :]


def _outproj_router(ys, w_out, mix_layer, h, g, wr, moe_tile):
    t = h.shape[0]
    wr_hi = wr.astype(BF16)
    wr_lo = (wr - wr_hi.astype(F32)).astype(BF16)
    rt = min(ROUTER_TILE, moe_tile)
    assert moe_tile % rt == 0 and rt % MOE_SUB == 0
    r = lax.broadcasted_iota(jnp.int32, (rt, rt), 0)
    c = lax.broadcasted_iota(jnp.int32, (rt, rt), 1)
    ltri = jnp.where(c <= r, 1.0, 0.0).astype(BF16)
    nsteps = t // rt
    subs = rt // MOE_SUB
    full = lambda a: pl.BlockSpec(a.shape, lambda i: (0, 0))
    row = lambda cdim: pl.BlockSpec((rt, cdim), lambda i: (i, 0))
    return pl.pallas_call(
        functools.partial(_router_kernel, subs_per_tile=moe_tile // rt),
        grid=(nsteps,),
        in_specs=_mix_specs(w_out, mix_layer, rt) + [full(g), full(wr_hi), full(wr_lo), full(ltri)],
        out_specs=[row(D_MODEL), row(D_MODEL), row(N_EXPERTS), row(N_EXPERTS),
                   pl.BlockSpec((subs, 1, N_EXPERTS), lambda i: (i, 0, 0))],
        out_shape=[jax.ShapeDtypeStruct((t, D_MODEL), F32),
                   jax.ShapeDtypeStruct((t, D_MODEL), BF16),
                   jax.ShapeDtypeStruct((t, N_EXPERTS), F32),
                   jax.ShapeDtypeStruct((t, N_EXPERTS), jnp.int32),
                   jax.ShapeDtypeStruct((t // MOE_SUB, 1, N_EXPERTS), jnp.int32)],
        scratch_shapes=[pltpu.VMEM((1, N_EXPERTS), F32), pltpu.VMEM((rt, D_MIX), BF16)],
        compiler_params=_params("arbitrary"),
        name="outproj_norm_router",
    )(*ys, w_out, h, g, wr_hi, wr_lo, ltri)


def _moe_kernel(cum_ref, x_ref, rank_ref, gate_ref, wg_ref, wu_ref, wd_ref, o_ref,
                xc_ref, yc_ref, tmp_ref):
    i, e, f = pl.program_id(0), pl.program_id(1), pl.program_id(2)
    tile = x_ref.shape[0]
    nsub = tile // MOE_SUB
    base = (i * N_EXPERTS + e) * (nsub + 1)
    nchunk = (cum_ref[base + nsub] + MOE_CHUNK - 1) // MOE_CHUNK

    @pl.when((e == 0) & (f == 0))
    def _():
        o_ref[...] = jnp.zeros_like(o_ref)

    def overlaps(c, s):
        return (cum_ref[base + s] < (c + 1) * MOE_CHUNK) & (cum_ref[base + s + 1] > c * MOE_CHUNK)

    def match(c, s):
        rows = c * MOE_CHUNK + lax.broadcasted_iota(jnp.int32, (MOE_CHUNK, MOE_SUB), 0)
        return rank_ref[:, s * MOE_SUB:(s + 1) * MOE_SUB] == rows

    @pl.when(f == 0)
    def _():
        def gather(c, _):
            tmp_ref[...] = jnp.zeros_like(tmp_ref)
            for s in range(nsub):
                @pl.when(overlaps(c, s))
                def _():
                    p = jnp.where(match(c, s), 1.0, 0.0).astype(BF16)
                    tmp_ref[...] += jnp.dot(p, x_ref[s * MOE_SUB:(s + 1) * MOE_SUB, :],
                                            preferred_element_type=F32)
            xc_ref[pl.ds(pl.multiple_of(c * MOE_CHUNK, MOE_CHUNK), MOE_CHUNK), :] = (
                tmp_ref[...].astype(BF16))
            return 0
        lax.fori_loop(0, nchunk, gather, 0)

    def ffn_rows(start, m):
        rows = pl.ds(start, m)
        y = _swiglu_partial(xc_ref[rows, :], wg_ref, wu_ref, wd_ref, MOE_FF_SUB)
        tail = -m % MOE_CHUNK

        @pl.when(f == 0)
        def _():
            yc_ref[rows, :] = y
            if tail:
                yc_ref[pl.ds(start + m, tail), :] = jnp.zeros((tail, D_MODEL), F32)

        @pl.when(f != 0)
        def _():
            yc_ref[rows, :] += y

    count = cum_ref[base + nsub]
    lo = 0
    for m in [v for v in MOE_GROUP_ROWS if v <= xc_ref.shape[0]]:
        @pl.when((count > lo) & (count <= m))
        def _():
            ffn_rows(0, m)
        lo = m

    @pl.when(count > lo)
    def _():
        ffn_rows(0, lo)

        def extra(c, _):
            ffn_rows(pl.multiple_of(c * MOE_CHUNK, MOE_CHUNK), MOE_CHUNK)
            return 0
        lax.fori_loop(lo // MOE_CHUNK, nchunk, extra, 0)

    @pl.when(f == pl.num_programs(2) - 1)
    def _():
        def scatter(c, _):
            y = yc_ref[pl.ds(pl.multiple_of(c * MOE_CHUNK, MOE_CHUNK), MOE_CHUNK), :].astype(BF16)
            for s in range(nsub):
                @pl.when(overlaps(c, s))
                def _():
                    gate = gate_ref[:, s * MOE_SUB:(s + 1) * MOE_SUB]
                    pg = jnp.where(match(c, s), gate, 0.0).astype(BF16)
                    upd = lax.dot_general(pg, y, (((0,), (0,)), ((), ())),
                                          preferred_element_type=F32)
                    rows = slice(s * MOE_SUB, (s + 1) * MOE_SUB)
                    o_ref[rows, :] = (o_ref[rows, :].astype(F32) + upd).astype(BF16)
            return 0
        lax.fori_loop(0, nchunk, scatter, 0)


def _moe(xn, gates, rank, cum_end, wg, wu, wd, layer, moe_tile):
    t = xn.shape[0]
    nf = D_FF // FF_TILE
    ntile = t // moe_tile
    nsub = moe_tile // MOE_SUB
    capacity = -(-moe_tile // MOE_CHUNK) * MOE_CHUNK
    to_rows = lambda a: a.reshape(ntile, moe_tile, N_EXPERTS).transpose(0, 2, 1).reshape(
        ntile, N_EXPERTS, 1, moe_tile)
    cum = cum_end.reshape(ntile, nsub, N_EXPERTS).transpose(0, 2, 1)
    cum = jnp.concatenate([jnp.zeros((ntile, N_EXPERTS, 1), jnp.int32), cum], axis=-1).reshape(-1)
    meta = pl.BlockSpec((None, None, 1, moe_tile), lambda i, e, f, cum: (i, e, 0, 0))
    grid_spec = pltpu.PrefetchScalarGridSpec(
        num_scalar_prefetch=1,
        grid=(ntile, N_EXPERTS, nf),
        in_specs=[pl.BlockSpec((moe_tile, D_MODEL), lambda i, e, f, cum: (i, 0)),
                  meta, meta,
                  pl.BlockSpec((None, None, D_MODEL, FF_TILE),
                               lambda i, e, f, cum: (layer, e, 0, f)),
                  pl.BlockSpec((None, None, D_MODEL, FF_TILE),
                               lambda i, e, f, cum: (layer, e, 0, f)),
                  pl.BlockSpec((None, None, FF_TILE, D_MODEL),
                               lambda i, e, f, cum: (layer, e, f, 0))],
        out_specs=pl.BlockSpec((moe_tile, D_MODEL), lambda i, e, f, cum: (i, 0)),
        scratch_shapes=[pltpu.VMEM((capacity, D_MODEL), BF16),
                        pltpu.VMEM((capacity, D_MODEL), F32),
                        pltpu.VMEM((MOE_CHUNK, D_MODEL), F32)])
    return pl.pallas_call(
        _moe_kernel,
        grid_spec=grid_spec,
        out_shape=jax.ShapeDtypeStruct((t, D_MODEL), BF16),
        compiler_params=_params("parallel", "arbitrary", "arbitrary"),
        name="moe_swiglu",
    )(cum, xn, to_rows(rank), to_rows(gates), wg, wu, wd)


def _final_norm_kernel(h_ref, y_ref, g_ref, o_ref):
    o_ref[...] = _rms_norm_rows(h_ref[...] + y_ref[...].astype(F32), g_ref[...])


def _final_norm(h, y, g):
    t = h.shape[0]
    row = pl.BlockSpec((ROW_TILE, D_MODEL), lambda i: (i, 0))
    return pl.pallas_call(
        _final_norm_kernel,
        grid=(t // ROW_TILE,),
        in_specs=[row, row, pl.BlockSpec(g.shape, lambda i: (0, 0))],
        out_specs=row,
        out_shape=jax.ShapeDtypeStruct((t, D_MODEL), F32),
        compiler_params=_params("parallel"),
        name="final_norm",
    )(h, y, g)


def _block_diag(w):
    n = w.shape[0]
    eye = jnp.eye(n, dtype=w.dtype)
    return jnp.einsum("hij,hg->higj", w, eye).reshape(n * HEAD_DIM, n * HEAD_DIM)


def kernel(x, mix_norm_g, w_in, lru_conv_w, lru_conv_b, lru_wa, lru_ba, lru_wx, lru_bx, lru_lam,
           sc_conv_w, mix_out_g, w_out, ffn_norm_g, dense_wg, dense_wu, dense_wd,
           router_w, moe_wg, moe_wu, moe_wd, final_norm_g):
    batch, seq, _ = x.shape
    depth = w_in.shape[0]
    assert seq % ATT_BLOCK == 0 and seq % LRU_CHUNK == 0 and (batch * seq) % ROW_TILE == 0
    h = x.reshape(batch * seq, D_MODEL)
    row = lambda v: v.reshape(1, -1)
    moe_tile = min(MOE_TILE, batch * seq)
    assert (batch * seq) % moe_tile == 0 and moe_tile % MOE_SUB == 0
    w_in, w_out, dense_wg, dense_wu, dense_wd, moe_wg, moe_wu, moe_wd = (
        w.astype(BF16) for w in (w_in, w_out, dense_wg, dense_wu, dense_wd, moe_wg, moe_wu, moe_wd))
    pending = None
    for l in range(depth):
        h, lru, qkv, sc = _inproj(h, pending, row(mix_norm_g[l]), w_in, l)
        wgate = jnp.concatenate([_block_diag(lru_wa[l]), _block_diag(lru_wx[l])], axis=1).astype(BF16)
        bgate = jnp.concatenate([lru_ba[l], lru_bx[l]]).reshape(1, -1)
        g_mix = row(mix_out_g[l])
        y_lru, y_sc = _lru_sc(lru, sc, lru_conv_w[l], row(lru_conv_b[l]), wgate, bgate,
                              row(lru_lam[l]), sc_conv_w[l], g_mix[:, 0:D_LRU],
                              g_mix[:, D_LRU + D_SB:], batch, seq)
        y_sb = _sb_attention(qkv, g_mix[:, D_LRU:D_LRU + D_SB], batch, seq)
        ys = (y_lru, y_sb, y_sc)
        j = l // 2
        if l % 2 == 0:
            h = _outproj_ffn(ys, w_out, l, h, row(ffn_norm_g[l]), dense_wg, dense_wu, dense_wd, j)
            pending = None
        else:
            h, xn, gates, rank, cum_end = _outproj_router(ys, w_out, l, h, row(ffn_norm_g[l]),
                                                          router_w[j], moe_tile)
            pending = _moe(xn, gates, rank, cum_end, moe_wg, moe_wu, moe_wd, j, moe_tile)
    if pending is None:
        pending = jnp.zeros(h.shape, BF16)
    return _final_norm(h, pending, row(final_norm_g)).reshape(batch, seq, D_MODEL)
```

```python
import functools

import jax
import jax.numpy as jnp
from jax import lax
from jax.experimental import pallas as pl
from jax.experimental.pallas import tpu as pltpu

F32 = jnp.float32
BF16 = jnp.bfloat16

D_MODEL = 1024
HEAD_DIM = 64
D_LRU = 384
D_SB = 384
D_SC = 256
D_MIX = D_LRU + D_SB + D_SC
LRU_CONV = 4
SC_CONV = 3
LRU_C = 8.0
D_FF = 2816
N_EXPERTS = 8
EPS = 1e-6
F32_EXP_ZERO = -104.0

V7X_VMEM_LIMIT_BYTES = 56 * 1024 * 1024
SUBLANES = 8
LANES = 128

ROW_TILE = 512
LRU_CHUNK = 256
ATT_BLOCK = 256
ATT_QBLOCKS = 2
FF_TILE = 1408
MOE_TILE = 2048
MOE_SUB = 512
ROUTER_TILE = 1024
MOE_CHUNK = 256
MOE_GROUP_ROWS = (256, 512, 576, 640, 768)
FF_SUB = 256
MOE_FF_SUB = 704


def _params(*sem):
    return pltpu.CompilerParams(dimension_semantics=sem, vmem_limit_bytes=V7X_VMEM_LIMIT_BYTES)


def _rms_norm_rows(x, g):
    ms = jnp.mean(x * x, axis=-1, keepdims=True)
    return x * lax.rsqrt(ms + EPS) * g


def _sigmoid(x):
    return 1.0 / (1.0 + jnp.exp(-x))


def _head_norm(y, g, avg):
    ms = jnp.dot((y * y).astype(BF16), avg, preferred_element_type=F32)
    return (y * lax.rsqrt(ms + EPS) * g).astype(BF16)


def _head_avg_matrix(width):
    r = lax.broadcasted_iota(jnp.int32, (width, width), 0) // HEAD_DIM
    c = lax.broadcasted_iota(jnp.int32, (width, width), 1) // HEAD_DIM
    return jnp.where(r == c, 1.0 / HEAD_DIM, 0.0).astype(BF16)


def _inproj_kernel(*refs, has_residual):
    if has_residual:
        h_ref, y_ref, g_ref, w_ref, hout_ref, lru_ref, qkv_ref, sc_ref = refs
        h = h_ref[...] + y_ref[...].astype(F32)
        hout_ref[...] = h
    else:
        h_ref, g_ref, w_ref, lru_ref, qkv_ref, sc_ref = refs
        h = h_ref[...]
    xn = _rms_norm_rows(h, g_ref[...]).astype(BF16)
    c0, c1 = 2 * D_LRU, 2 * D_LRU + 3 * D_SB
    lru_ref[...] = jnp.dot(xn, w_ref[:, 0:c0], preferred_element_type=F32).astype(BF16)
    qkv_ref[...] = jnp.dot(xn, w_ref[:, c0:c1], preferred_element_type=F32).astype(BF16)
    sc_ref[...] = jnp.dot(xn, w_ref[:, c1:], preferred_element_type=F32).astype(BF16)


def _inproj(h, y, g, w_all, layer):
    t = h.shape[0]
    d_in = w_all.shape[2]
    row = lambda c: pl.BlockSpec((ROW_TILE, c), lambda i: (i, 0))
    acts = [h] if y is None else [h, y]
    proj_shapes = [jax.ShapeDtypeStruct((t, 2 * D_LRU), BF16),
                   jax.ShapeDtypeStruct((t, 3 * D_SB), BF16),
                   jax.ShapeDtypeStruct((t, d_in - 2 * D_LRU - 3 * D_SB), BF16)]
    proj_specs = [row(2 * D_LRU), row(3 * D_SB), row(3 * D_SC)]
    out = pl.pallas_call(
        functools.partial(_inproj_kernel, has_residual=y is not None),
        grid=(t // ROW_TILE,),
        in_specs=[row(D_MODEL)] * len(acts) + [
            pl.BlockSpec(g.shape, lambda i: (0, 0)),
            pl.BlockSpec((None, D_MODEL, d_in), lambda i: (layer, 0, 0))],
        out_specs=([] if y is None else [row(D_MODEL)]) + proj_specs,
        out_shape=([] if y is None else [jax.ShapeDtypeStruct((t, D_MODEL), F32)]) + proj_shapes,
        compiler_params=_params("parallel"),
        name="norm_inproj",
    )(*acts, g, w_all)
    return (h, *out) if y is None else tuple(out)


def _lru_sc_kernel(lru_ref, sc_ref, *refs):
    params, (ylru_ref, ysc_ref), scratch = refs[:10], refs[10:12], refs[12:]
    xhalo, phalo, abuf, bbuf, hcar = scratch
    pad = lru_ref.shape[1] // 2

    @pl.when(pl.program_id(0) == 0)
    def _():
        xhalo[...] = jnp.zeros_like(xhalo)
        phalo[...] = jnp.zeros_like(phalo)
        hcar[...] = jnp.zeros_like(hcar)
        abuf[:, 0:pad, :] = jnp.ones((abuf.shape[0], pad, D_LRU), F32)
        bbuf[:, 0:pad, :] = jnp.zeros((bbuf.shape[0], pad, D_LRU), F32)

    for r in range(lru_ref.shape[0]):
        _lru_sc_chain(lru_ref.at[r], sc_ref.at[r], *params, ylru_ref.at[r], ysc_ref.at[r],
                      *(buf.at[r] for buf in scratch))


def _lru_sc_chain(lru_ref, sc_ref, cw_ref, cb_ref, wgate_ref, bgate_ref, lam_ref, scw_ref,
                  glru_ref, gsc_ref, avg_ref, shift_ref,
                  ylru_ref, ysc_ref, xhalo, phalo, abuf, bbuf, hcar):
    tc = lru_ref.shape[0]
    pad = tc // 2

    def shifted(x_b, d):
        return jnp.dot(shift_ref[d - 1], x_b, preferred_element_type=F32)

    def with_halo(y, corr):
        return jnp.concatenate([y[0:SUBLANES, :] + corr, y[SUBLANES:, :]], axis=0)

    x_b = lru_ref[:, 0:D_LRU]
    conv = cb_ref[...] + cw_ref[LRU_CONV - 1:LRU_CONV, :] * x_b.astype(F32)
    corr = jnp.zeros((SUBLANES, D_LRU), F32)
    for d in range(1, LRU_CONV):
        w = cw_ref[LRU_CONV - 1 - d:LRU_CONV - d, :]
        conv = conv + w * shifted(x_b, d)
        corr = corr + w * xhalo[pl.ds(SUBLANES - d, SUBLANES), :]
    conv = with_halo(conv, corr)
    xhalo[0:SUBLANES, :] = x_b[tc - SUBLANES:tc, :].astype(F32)

    gates = jnp.dot(conv.astype(BF16), wgate_ref[...], preferred_element_type=F32) + bgate_ref[...]
    gate_r = _sigmoid(gates[:, 0:D_LRU])
    gate_i = _sigmoid(gates[:, D_LRU:])
    lam = lam_ref[...]
    log_sig_lam = jnp.minimum(lam, 0.0) - jnp.log1p(jnp.exp(-jnp.abs(lam)))
    log_a = (LRU_C * gate_r) * log_sig_lam
    a = jnp.exp(log_a)
    v = 1.0 - a * a
    u = jnp.where(v > 0.0, v * lax.rsqrt(v), 0.0) * (gate_i * conv)

    s = 1
    while s < tc:
        abuf[pad:pad + tc, :] = a
        bbuf[pad:pad + tc, :] = u
        a_sh = abuf[pl.ds(pad - s, tc), :]
        u_sh = bbuf[pl.ds(pad - s, tc), :]
        u = a * u_sh + u
        a = a * a_sh
        s *= 2
    h = a * hcar[0:1, :] + u
    hcar[...] = jnp.broadcast_to(h[tc - 1:tc, :], hcar.shape)

    ylru_ref[...] = _head_norm(h * jax.nn.gelu(lru_ref[:, D_LRU:].astype(F32)), glru_ref[...],
                               avg_ref[...])

    c_b = sc_ref[:, D_SC:2 * D_SC]
    s_b = sc_ref[:, 2 * D_SC:]
    p = c_b.astype(F32) * s_b.astype(F32)
    acc = scw_ref[SC_CONV - 1:SC_CONV, :] * p
    corr = jnp.zeros((SUBLANES, D_SC), F32)
    for d in range(1, SC_CONV):
        w = scw_ref[SC_CONV - 1 - d:SC_CONV - d, :]
        acc = acc + w * (shifted(c_b, d) * shifted(s_b, d))
        corr = corr + w * phalo[pl.ds(SUBLANES - d, SUBLANES), :]
    acc = with_halo(acc, corr)
    phalo[0:SUBLANES, :] = p[tc - SUBLANES:tc, :]
    ysc_ref[...] = _head_norm(sc_ref[:, 0:D_SC].astype(F32) * acc, gsc_ref[...],
                              avg_ref[0:D_SC, 0:D_SC])


def _lru_sc(lru, sc, cw, cb, wgate, bgate, lam, scw, g_lru, g_sc, batch, seq):
    tc = LRU_CHUNK
    nt = seq // tc
    avg = _head_avg_matrix(D_LRU)
    t_out = lax.broadcasted_iota(jnp.int32, (LRU_CONV - 1, tc, tc), 1)
    t_in = lax.broadcasted_iota(jnp.int32, (LRU_CONV - 1, tc, tc), 2)
    delay = lax.broadcasted_iota(jnp.int32, (LRU_CONV - 1, tc, tc), 0) + 1
    shift = jnp.where(t_in == t_out - delay, 1.0, 0.0).astype(BF16)
    row = lambda c: pl.BlockSpec((batch, tc, c), lambda t: (0, t, 0))
    full = lambda a: pl.BlockSpec(a.shape, lambda t: (0,) * a.ndim)
    per_row = lambda rows, c: pltpu.VMEM((batch, rows, c), F32)
    y_lru, y_sc = pl.pallas_call(
        _lru_sc_kernel,
        grid=(nt,),
        in_specs=[row(2 * D_LRU), row(3 * D_SC), full(cw), full(cb), full(wgate), full(bgate),
                  full(lam), full(scw), full(g_lru), full(g_sc), full(avg), full(shift)],
        out_specs=[row(D_LRU), row(D_SC)],
        out_shape=[jax.ShapeDtypeStruct((batch, seq, D_LRU), BF16),
                   jax.ShapeDtypeStruct((batch, seq, D_SC), BF16)],
        scratch_shapes=[per_row(2 * SUBLANES, D_LRU), per_row(2 * SUBLANES, D_SC),
                        per_row(tc + tc // 2, D_LRU), per_row(tc + tc // 2, D_LRU),
                        per_row(SUBLANES, D_LRU)],
        compiler_params=_params("arbitrary"),
        name="lru_shortconv",
    )(lru.reshape(batch, seq, -1), sc.reshape(batch, seq, -1), cw, cb, wgate, bgate, lam, scw,
      g_lru, g_sc, avg, shift)
    return y_lru.reshape(batch * seq, D_LRU), y_sc.reshape(batch * seq, D_SC)


def _sb_attn_kernel(q_ref, k_ref, v_ref, cm_ref, g_ref, avg_ref, o_ref, o_acc, c_acc):
    blk = cm_ref.shape[0]
    nqb = q_ref.shape[0] // blk
    npair = q_ref.shape[1] // LANES
    first = pl.program_id(1) * nqb
    rows = lambda b: slice(b * blk, (b + 1) * blk)
    head0 = lax.broadcasted_iota(jnp.int32, (1, LANES), 1) < HEAD_DIM
    lanes = lambda p: slice(p * LANES, (p + 1) * LANES)

    def stacked_q(b, p):
        q = q_ref[rows(b), lanes(p)] * jnp.asarray(HEAD_DIM ** -0.5, BF16)
        zero = jnp.zeros_like(q)
        return jnp.concatenate([jnp.where(head0, q, zero), jnp.where(head0, zero, q)], axis=0)

    def step(p, qq, j, o, carry, strict_mask=None, valid=None):
        start = pl.multiple_of(j * blk, blk)
        kt = k_ref[pl.ds(start, blk), lanes(p)]
        vt = v_ref[pl.ds(start, blk), lanes(p)]
        z = lax.dot_general(qq, kt, (((1,), (1,)), ((), ())), preferred_element_type=F32)
        neg_abs = lax.bitcast_convert_type(
            lax.bitcast_convert_type(z, jnp.uint32) | jnp.uint32(0x80000000), F32)
        soft = jnp.log(1.0 + jnp.exp(neg_abs))
        log_beta = jnp.minimum(z, 0.0) - soft
        log_keep = log_beta - z
        keep = strict_mask if valid is None else valid
        if keep is not None:
            log_keep = jnp.where(keep, log_keep, 0.0)
        cs = jnp.dot(log_keep.astype(BF16), cm_ref[...], preferred_element_type=F32)
        later = cs[:, 0:blk] + jnp.concatenate([carry] * (blk // LANES), axis=1)
        w = jnp.exp(log_beta + later)
        if keep is not None:
            w = jnp.where(keep, w, 0.0)
        o = o + jnp.dot(w.astype(BF16), vt, preferred_element_type=F32)
        return o, carry + cs[:, blk:]

    row = lax.broadcasted_iota(jnp.int32, (2 * blk, blk), 0)
    col = lax.broadcasted_iota(jnp.int32, (2 * blk, blk), 1)
    strict = col < jnp.where(row >= blk, row - blk, row)
    zeros = jnp.zeros((2 * blk, LANES), F32)
    qqs = [[stacked_q(b, p) for p in range(npair)] for b in range(nqb)]
    lives = []
    for b in range(nqb):
        i = first + b
        live = None
        for p in range(npair):
            o, carry = step(p, qqs[b][p], i, zeros, zeros, strict_mask=strict)
            o, carry = step(p, qqs[b][p], jnp.maximum(i - 1, 0), o, carry, valid=i > 0)
            o_acc[b * npair + p] = o
            c_acc[b * npair + p] = carry
            top = jnp.max(carry)
            live = top if live is None else jnp.maximum(live, top)
        lives.append(live)

    for b in range(nqb):
        i = first + b

        def cond(state, i=i):
            n, live = state
            return jnp.logical_and(n < i, live > F32_EXP_ZERO)

        def body(state, i=i, b=b):
            n, _ = state
            live = None
            for p in range(npair):
                c = b * npair + p
                o, carry = step(p, qqs[b][p], i - 1 - n, o_acc[c], c_acc[c])
                o_acc[c] = o
                c_acc[c] = carry
                top = jnp.max(carry)
                live = top if live is None else jnp.maximum(live, top)
            return n + 1, live

        lax.while_loop(cond, body, (jnp.int32(1), lives[b]))

    for b in range(nqb):
        for p in range(npair):
            o = o_acc[b * npair + p]
            o_ref[rows(b), lanes(p)] = _head_norm(jnp.where(head0, o[0:blk, :], o[blk:, :]),
                                                  g_ref[:, lanes(p)], avg_ref[...])


def _cumsum_matrix(blk):
    j = lax.broadcasted_iota(jnp.int32, (blk, blk + LANES), 0)
    s = lax.broadcasted_iota(jnp.int32, (blk, blk + LANES), 1)
    return jnp.where((j > s) | (s >= blk), 1.0, 0.0).astype(BF16)


def _sb_attention(qkv, g_sb, batch, seq):
    blk = ATT_BLOCK
    rows = ATT_QBLOCKS * blk
    nq = seq // rows
    npair = D_SB // LANES
    cm = _cumsum_matrix(blk)
    avg = _head_avg_matrix(LANES)
    return pl.pallas_call(
        _sb_attn_kernel,
        grid=(batch, nq),
        in_specs=[pl.BlockSpec((rows, D_SB), lambda b, i: (b * nq + i, 0)),
                  pl.BlockSpec((seq, D_SB), lambda b, i: (b, 1)),
                  pl.BlockSpec((seq, D_SB), lambda b, i: (b, 2)),
                  pl.BlockSpec(cm.shape, lambda b, i: (0, 0)),
                  pl.BlockSpec(g_sb.shape, lambda b, i: (0, 0)),
                  pl.BlockSpec(avg.shape, lambda b, i: (0, 0))],
        out_specs=pl.BlockSpec((rows, D_SB), lambda b, i: (b * nq + i, 0)),
        out_shape=jax.ShapeDtypeStruct((batch * seq, D_SB), BF16),
        scratch_shapes=[pltpu.VMEM((ATT_QBLOCKS * npair, 2 * blk, LANES), F32),
                        pltpu.VMEM((ATT_QBLOCKS * npair, 2 * blk, LANES), F32)],
        compiler_params=_params("parallel", "arbitrary"),
        name="stickbreak_attn",
    )(qkv, qkv, qkv, cm, g_sb, avg)


def _mix_residual(ylru_ref, ysb_ref, ysc_ref, w_ref, h_ref, ycat_ref):
    lo = 0
    for y_ref in (ylru_ref, ysb_ref, ysc_ref):
        c = y_ref.shape[1]
        ycat_ref[:, lo:lo + c] = y_ref[...]
        lo += c
    return h_ref[...] + jnp.dot(ycat_ref[...], w_ref[...], preferred_element_type=F32)


def _mix_specs(w_out_all, layer, rows):
    row = lambda cdim: pl.BlockSpec((rows, cdim), lambda i: (i, 0))
    return [row(D_LRU), row(D_SB), row(D_SC),
            pl.BlockSpec((None,) + w_out_all.shape[1:], lambda i: (layer, 0, 0)), row(D_MODEL)]


def _swiglu_partial(xn, wg_ref, wu_ref, wd_ref, sub):
    width = wg_ref.shape[1]
    out = None
    for lo in range(0, width, sub):
        hi = min(lo + sub, width)
        a = jnp.dot(xn, wg_ref[:, lo:hi], preferred_element_type=F32)
        b = jnp.dot(xn, wu_ref[:, lo:hi], preferred_element_type=F32)
        hid = (a * _sigmoid(a) * b).astype(BF16)
        part = jnp.dot(hid, wd_ref[lo:hi, :], preferred_element_type=F32)
        out = part if out is None else out + part
    return out


def _ffn_kernel(ylru_ref, ysb_ref, ysc_ref, wout_ref, h_ref, g_ref, wg_ref, wu_ref, wd_ref, o_ref,
                ycat_ref):
    h = _mix_residual(ylru_ref, ysb_ref, ysc_ref, wout_ref, h_ref, ycat_ref)
    xn = _rms_norm_rows(h, g_ref[...]).astype(BF16)
    o_ref[...] = h + _swiglu_partial(xn, wg_ref, wu_ref, wd_ref, FF_SUB)


def _outproj_ffn(ys, w_out, mix_layer, h, g, wg, wu, wd, layer):
    t = h.shape[0]
    full = lambda a: pl.BlockSpec(a.shape, lambda i: (0, 0))
    stacked = lambda a: pl.BlockSpec((None,) + a.shape[1:], lambda i: (layer, 0, 0))
    return pl.pallas_call(
        _ffn_kernel,
        grid=(t // ROW_TILE,),
        in_specs=_mix_specs(w_out, mix_layer, ROW_TILE) + [full(g), stacked(wg), stacked(wu),
                                                            stacked(wd)],
        out_specs=pl.BlockSpec((ROW_TILE, D_MODEL), lambda i: (i, 0)),
        out_shape=jax.ShapeDtypeStruct((t, D_MODEL), F32),
        scratch_shapes=[pltpu.VMEM((ROW_TILE, D_MIX), BF16)],
        compiler_params=_params("parallel"),
        name="outproj_norm_swiglu",
    )(*ys, w_out, h, g, wg, wu, wd)


def _top2_gates(logits):
    lane = lax.broadcasted_iota(jnp.int32, logits.shape, 1)
    m1 = jnp.max(logits, axis=-1, keepdims=True)
    i1 = jnp.min(jnp.where(logits == m1, lane, N_EXPERTS), axis=-1, keepdims=True)
    first = lane == i1
    rest = jnp.where(first, -jnp.inf, logits)
    m2 = jnp.max(rest, axis=-1, keepdims=True)
    i2 = jnp.min(jnp.where(rest == m2, lane, N_EXPERTS), axis=-1, keepdims=True)
    e2 = jnp.exp(m2 - m1)
    w1 = 1.0 / (1.0 + e2)
    return jnp.where(first, w1, jnp.where(lane == i2, e2 * w1, 0.0))


def _router_logits(xn_f32, wr_hi, wr_lo):
    x_hi = xn_f32.astype(BF16)
    x_lo = (xn_f32 - x_hi.astype(F32)).astype(BF16)
    return (jnp.dot(x_hi, wr_hi, preferred_element_type=F32)
            + jnp.dot(x_hi, wr_lo, preferred_element_type=F32)
            + jnp.dot(x_lo, wr_hi, preferred_element_type=F32))


def _router_kernel(ylru_ref, ysb_ref, ysc_ref, wout_ref, h_ref, g_ref, wrh_ref, wrl_ref, ltri_ref,
                   hmid_ref, xn_ref, gate_ref, rank_ref, cum_ref, base_ref, ycat_ref, *,
                   subs_per_tile):
    @pl.when(pl.program_id(0) % subs_per_tile == 0)
    def _():
        base_ref[...] = jnp.zeros_like(base_ref)

    h = _mix_residual(ylru_ref, ysb_ref, ysc_ref, wout_ref, h_ref, ycat_ref)
    hmid_ref[...] = h
    xn = _rms_norm_rows(h, g_ref[...])
    xn_ref[...] = xn.astype(BF16)
    gates = _top2_gates(_router_logits(xn, wrh_ref[...], wrl_ref[...]))
    gate_ref[...] = gates
    sel = gates > 0.0
    inc = jnp.dot(ltri_ref[...], jnp.where(sel, 1.0, 0.0).astype(BF16), preferred_element_type=F32)
    count = base_ref[...] + inc
    rank_ref[...] = jnp.where(sel, count - 1.0, -1.0).astype(jnp.int32)
    rows = count.shape[0]
    for s in range(rows // MOE_SUB):
        cum_ref[s] = count[(s + 1) * MOE_SUB - 1:(s + 1) * MOE_SUB, :].astype(jnp.int32)
    base_ref[...] = count[rows - 1:rows, :]


def _outproj_router(ys, w_out, mix_layer, h, g, wr, moe_tile):
    t = h.shape[0]
    wr_hi = wr.astype(BF16)
    wr_lo = (wr - wr_hi.astype(F32)).astype(BF16)
    rt = min(ROUTER_TILE, moe_tile)
    assert moe_tile % rt == 0 and rt % MOE_SUB == 0
    r = lax.broadcasted_iota(jnp.int32, (rt, rt), 0)
    c = lax.broadcasted_iota(jnp.int32, (rt, rt), 1)
    ltri = jnp.where(c <= r, 1.0, 0.0).astype(BF16)
    nsteps = t // rt
    subs = rt // MOE_SUB
    full = lambda a: pl.BlockSpec(a.shape, lambda i: (0, 0))
    row = lambda cdim: pl.BlockSpec((rt, cdim), lambda i: (i, 0))
    return pl.pallas_call(
        functools.partial(_router_kernel, subs_per_tile=moe_tile // rt),
        grid=(nsteps,),
        in_specs=_mix_specs(w_out, mix_layer, rt) + [full(g), full(wr_hi), full(wr_lo), full(ltri)],
        out_specs=[row(D_MODEL), row(D_MODEL), row(N_EXPERTS), row(N_EXPERTS),
                   pl.BlockSpec((subs, 1, N_EXPERTS), lambda i: (i, 0, 0))],
        out_shape=[jax.ShapeDtypeStruct((t, D_MODEL), F32),
                   jax.ShapeDtypeStruct((t, D_MODEL), BF16),
                   jax.ShapeDtypeStruct((t, N_EXPERTS), F32),
                   jax.ShapeDtypeStruct((t, N_EXPERTS), jnp.int32),
                   jax.ShapeDtypeStruct((t // MOE_SUB, 1, N_EXPERTS), jnp.int32)],
        scratch_shapes=[pltpu.VMEM((1, N_EXPERTS), F32), pltpu.VMEM((rt, D_MIX), BF16)],
        compiler_params=_params("arbitrary"),
        name="outproj_norm_router",
    )(*ys, w_out, h, g, wr_hi, wr_lo, ltri)


def _moe_kernel(cum_ref, x_ref, rank_ref, gate_ref, wg_ref, wu_ref, wd_ref, o_ref,
                xc_ref, yc_ref, tmp_ref):
    i, e, f = pl.program_id(0), pl.program_id(1), pl.program_id(2)
    tile = x_ref.shape[0]
    nsub = tile // MOE_SUB
    base = (i * N_EXPERTS + e) * (nsub + 1)
    nchunk = (cum_ref[base + nsub] + MOE_CHUNK - 1) // MOE_CHUNK

    @pl.when((e == 0) & (f == 0))
    def _():
        o_ref[...] = jnp.zeros_like(o_ref)

    def overlaps(c, s):
        return (cum_ref[base + s] < (c + 1) * MOE_CHUNK) & (cum_ref[base + s + 1] > c * MOE_CHUNK)

    def match(c, s):
        rows = c * MOE_CHUNK + lax.broadcasted_iota(jnp.int32, (MOE_CHUNK, MOE_SUB), 0)
        return rank_ref[:, s * MOE_SUB:(s + 1) * MOE_SUB] == rows

    @pl.when(f == 0)
    def _():
        def gather(c, _):
            tmp_ref[...] = jnp.zeros_like(tmp_ref)
            for s in range(nsub):
                @pl.when(overlaps(c, s))
                def _():
                    p = jnp.where(match(c, s), 1.0, 0.0).astype(BF16)
                    tmp_ref[...] += jnp.dot(p, x_ref[s * MOE_SUB:(s + 1) * MOE_SUB, :],
                                            preferred_element_type=F32)
            xc_ref[pl.ds(pl.multiple_of(c * MOE_CHUNK, MOE_CHUNK), MOE_CHUNK), :] = (
                tmp_ref[...].astype(BF16))
            return 0
        lax.fori_loop(0, nchunk, gather, 0)

    def ffn_rows(start, m):
        rows = pl.ds(start, m)
        y = _swiglu_partial(xc_ref[rows, :], wg_ref, wu_ref, wd_ref, MOE_FF_SUB)
        tail = -m % MOE_CHUNK

        @pl.when(f == 0)
        def _():
            yc_ref[rows, :] = y
            if tail:
                yc_ref[pl.ds(start + m, tail), :] = jnp.zeros((tail, D_MODEL), F32)

        @pl.when(f != 0)
        def _():
            yc_ref[rows, :] += y

    count = cum_ref[base + nsub]
    lo = 0
    for m in [v for v in MOE_GROUP_ROWS if v <= xc_ref.shape[0]]:
        @pl.when((count > lo) & (count <= m))
        def _():
            ffn_rows(0, m)
        lo = m

    @pl.when(count > lo)
    def _():
        ffn_rows(0, lo)

        def extra(c, _):
            ffn_rows(pl.multiple_of(c * MOE_CHUNK, MOE_CHUNK), MOE_CHUNK)
            return 0
        lax.fori_loop(lo // MOE_CHUNK, nchunk, extra, 0)

    @pl.when(f == pl.num_programs(2) - 1)
    def _():
        def scatter(c, _):
            y = yc_ref[pl.ds(pl.multiple_of(c * MOE_CHUNK, MOE_CHUNK), MOE_CHUNK), :].astype(BF16)
            for s in range(nsub):
                @pl.when(overlaps(c, s))
                def _():
                    gate = gate_ref[:, s * MOE_SUB:(s + 1) * MOE_SUB]
                    pg = jnp.where(match(c, s), gate, 0.0).astype(BF16)
                    upd = lax.dot_general(pg, y, (((0,), (0,)), ((), ())),
                                          preferred_element_type=F32)
                    rows = slice(s * MOE_SUB, (s + 1) * MOE_SUB)
                    o_ref[rows, :] = (o_ref[rows, :].astype(F32) + upd).astype(BF16)
            return 0
        lax.fori_loop(0, nchunk, scatter, 0)


def _moe(xn, gates, rank, cum_end, wg, wu, wd, layer, moe_tile):
    t = xn.shape[0]
    nf = D_FF // FF_TILE
    ntile = t // moe_tile
    nsub = moe_tile // MOE_SUB
    capacity = -(-moe_tile // MOE_CHUNK) * MOE_CHUNK
    to_rows = lambda a: a.reshape(ntile, moe_tile, N_EXPERTS).transpose(0, 2, 1).reshape(
        ntile, N_EXPERTS, 1, moe_tile)
    cum = cum_end.reshape(ntile, nsub, N_EXPERTS).transpose(0, 2, 1)
    cum = jnp.concatenate([jnp.zeros((ntile, N_EXPERTS, 1), jnp.int32), cum], axis=-1).reshape(-1)
    meta = pl.BlockSpec((None, None, 1, moe_tile), lambda i, e, f, cum: (i, e, 0, 0))
    grid_spec = pltpu.PrefetchScalarGridSpec(
        num_scalar_prefetch=1,
        grid=(ntile, N_EXPERTS, nf),
        in_specs=[pl.BlockSpec((moe_tile, D_MODEL), lambda i, e, f, cum: (i, 0)),
                  meta, meta,
                  pl.BlockSpec((None, None, D_MODEL, FF_TILE),
                               lambda i, e, f, cum: (layer, e, 0, f)),
                  pl.BlockSpec((None, None, D_MODEL, FF_TILE),
                               lambda i, e, f, cum: (layer, e, 0, f)),
                  pl.BlockSpec((None, None, FF_TILE, D_MODEL),
                               lambda i, e, f, cum: (layer, e, f, 0))],
        out_specs=pl.BlockSpec((moe_tile, D_MODEL), lambda i, e, f, cum: (i, 0)),
        scratch_shapes=[pltpu.VMEM((capacity, D_MODEL), BF16),
                        pltpu.VMEM((capacity, D_MODEL), F32),
                        pltpu.VMEM((MOE_CHUNK, D_MODEL), F32)])
    return pl.pallas_call(
        _moe_kernel,
        grid_spec=grid_spec,
        out_shape=jax.ShapeDtypeStruct((t, D_MODEL), BF16),
        compiler_params=_params("parallel", "arbitrary", "arbitrary"),
        name="moe_swiglu",
    )(cum, xn, to_rows(rank), to_rows(gates), wg, wu, wd)


def _final_norm_kernel(h_ref, y_ref, g_ref, o_ref):
    o_ref[...] = _rms_norm_rows(h_ref[...] + y_ref[...].astype(F32), g_ref[...])


def _final_norm(h, y, g):
    t = h.shape[0]
    row = pl.BlockSpec((ROW_TILE, D_MODEL), lambda i: (i, 0))
    return pl.pallas_call(
        _final_norm_kernel,
        grid=(t // ROW_TILE,),
        in_specs=[row, row, pl.BlockSpec(g.shape, lambda i: (0, 0))],
        out_specs=row,
        out_shape=jax.ShapeDtypeStruct((t, D_MODEL), F32),
        compiler_params=_params("parallel"),
        name="final_norm",
    )(h, y, g)


def _block_diag(w):
    n = w.shape[0]
    eye = jnp.eye(n, dtype=w.dtype)
    return jnp.einsum("hij,hg->higj", w, eye).reshape(n * HEAD_DIM, n * HEAD_DIM)


def kernel(x, mix_norm_g, w_in, lru_conv_w, lru_conv_b, lru_wa, lru_ba, lru_wx, lru_bx, lru_lam,
           sc_conv_w, mix_out_g, w_out, ffn_norm_g, dense_wg, dense_wu, dense_wd,
           router_w, moe_wg, moe_wu, moe_wd, final_norm_g):
    batch, seq, _ = x.shape
    depth = w_in.shape[0]
    assert seq % (ATT_BLOCK * ATT_QBLOCKS) == 0 and seq % LRU_CHUNK == 0
    assert (batch * seq) % ROW_TILE == 0
    h = x.reshape(batch * seq, D_MODEL)
    row = lambda v: v.reshape(1, -1)
    moe_tile = min(MOE_TILE, batch * seq)
    assert (batch * seq) % moe_tile == 0 and moe_tile % MOE_SUB == 0
    w_in, w_out, dense_wg, dense_wu, dense_wd, moe_wg, moe_wu, moe_wd = (
        w.astype(BF16) for w in (w_in, w_out, dense_wg, dense_wu, dense_wd, moe_wg, moe_wu, moe_wd))
    pending = None
    for l in range(depth):
        h, lru, qkv, sc = _inproj(h, pending, row(mix_norm_g[l]), w_in, l)
        wgate = jnp.concatenate([_block_diag(lru_wa[l]), _block_diag(lru_wx[l])], axis=1).astype(BF16)
        bgate = jnp.concatenate([lru_ba[l], lru_bx[l]]).reshape(1, -1)
        g_mix = row(mix_out_g[l])
        y_lru, y_sc = _lru_sc(lru, sc, lru_conv_w[l], row(lru_conv_b[l]), wgate, bgate,
                              row(lru_lam[l]), sc_conv_w[l], g_mix[:, 0:D_LRU],
                              g_mix[:, D_LRU + D_SB:], batch, seq)
        y_sb = _sb_attention(qkv, g_mix[:, D_LRU:D_LRU + D_SB], batch, seq)
        ys = (y_lru, y_sb, y_sc)
        j = l // 2
        if l % 2 == 0:
            h = _outproj_ffn(ys, w_out, l, h, row(ffn_norm_g[l]), dense_wg, dense_wu, dense_wd, j)
            pending = None
        else:
            h, xn, gates, rank, cum_end = _outproj_router(ys, w_out, l, h, row(ffn_norm_g[l]),
                                                          router_w[j], moe_tile)
            pending = _moe(xn, gates, rank, cum_end, moe_wg, moe_wu, moe_wd, j, moe_tile)
    if pending is None:
        pending = jnp.zeros(h.shape, BF16)
    return _final_norm(h, pending, row(final_norm_g)).reshape(batch, seq, D_MODEL)
```

```python
import functools

import jax
import jax.numpy as jnp
from jax import lax
from jax.experimental import pallas as pl
from jax.experimental.pallas import tpu as pltpu

F32 = jnp.float32
BF16 = jnp.bfloat16

D_MODEL = 1024
HEAD_DIM = 64
D_LRU = 384
D_SB = 384
D_SC = 256
D_MIX = D_LRU + D_SB + D_SC
LRU_CONV = 4
SC_CONV = 3
LRU_C = 8.0
D_FF = 2816
N_EXPERTS = 8
EPS = 1e-6
F32_EXP_ZERO = -104.0

V7X_VMEM_LIMIT_BYTES = 56 * 1024 * 1024
SUBLANES = 8
LANES = 128

ROW_TILE = 512
LRU_CHUNK = 256
ATT_BLOCK = 256
ATT_QBLOCKS = 4
FF_TILE = 1408
MOE_TILE = 2048
MOE_SUB = 512
ROUTER_TILE = 1024
MOE_CHUNK = 256
MOE_GROUP_ROWS = (256, 512, 576, 640, 768)
FF_SUB = 256
MOE_FF_SUB = 704


def _params(*sem):
    return pltpu.CompilerParams(dimension_semantics=sem, vmem_limit_bytes=V7X_VMEM_LIMIT_BYTES)


def _rms_norm_rows(x, g):
    ms = jnp.mean(x * x, axis=-1, keepdims=True)
    return x * lax.rsqrt(ms + EPS) * g


def _sigmoid(x):
    return 1.0 / (1.0 + jnp.exp(-x))


def _head_norm(y, g, avg):
    ms = jnp.dot((y * y).astype(BF16), avg, preferred_element_type=F32)
    return (y * lax.rsqrt(ms + EPS) * g).astype(BF16)


def _head_avg_matrix(width):
    r = lax.broadcasted_iota(jnp.int32, (width, width), 0) // HEAD_DIM
    c = lax.broadcasted_iota(jnp.int32, (width, width), 1) // HEAD_DIM
    return jnp.where(r == c, 1.0 / HEAD_DIM, 0.0).astype(BF16)


def _inproj_kernel(*refs, has_residual):
    if has_residual:
        h_ref, y_ref, g_ref, w_ref, hout_ref, lru_ref, qkv_ref, sc_ref = refs
        h = h_ref[...] + y_ref[...].astype(F32)
        hout_ref[...] = h
    else:
        h_ref, g_ref, w_ref, lru_ref, qkv_ref, sc_ref = refs
        h = h_ref[...]
    xn = _rms_norm_rows(h, g_ref[...]).astype(BF16)
    c0, c1 = 2 * D_LRU, 2 * D_LRU + 3 * D_SB
    lru_ref[...] = jnp.dot(xn, w_ref[:, 0:c0], preferred_element_type=F32).astype(BF16)
    qkv_ref[...] = jnp.dot(xn, w_ref[:, c0:c1], preferred_element_type=F32).astype(BF16)
    sc_ref[...] = jnp.dot(xn, w_ref[:, c1:], preferred_element_type=F32).astype(BF16)


def _inproj(h, y, g, w_all, layer):
    t = h.shape[0]
    d_in = w_all.shape[2]
    row = lambda c: pl.BlockSpec((ROW_TILE, c), lambda i: (i, 0))
    acts = [h] if y is None else [h, y]
    proj_shapes = [jax.ShapeDtypeStruct((t, 2 * D_LRU), BF16),
                   jax.ShapeDtypeStruct((t, 3 * D_SB), BF16),
                   jax.ShapeDtypeStruct((t, d_in - 2 * D_LRU - 3 * D_SB), BF16)]
    proj_specs = [row(2 * D_LRU), row(3 * D_SB), row(3 * D_SC)]
    out = pl.pallas_call(
        functools.partial(_inproj_kernel, has_residual=y is not None),
        grid=(t // ROW_TILE,),
        in_specs=[row(D_MODEL)] * len(acts) + [
            pl.BlockSpec(g.shape, lambda i: (0, 0)),
            pl.BlockSpec((None, D_MODEL, d_in), lambda i: (layer, 0, 0))],
        out_specs=([] if y is None else [row(D_MODEL)]) + proj_specs,
        out_shape=([] if y is None else [jax.ShapeDtypeStruct((t, D_MODEL), F32)]) + proj_shapes,
        compiler_params=_params("parallel"),
        name="norm_inproj",
    )(*acts, g, w_all)
    return (h, *out) if y is None else tuple(out)


def _lru_sc_kernel(lru_ref, sc_ref, *refs):
    params, (ylru_ref, ysc_ref), scratch = refs[:10], refs[10:12], refs[12:]
    xhalo, phalo, abuf, bbuf, hcar = scratch
    pad = lru_ref.shape[1] // 2

    @pl.when(pl.program_id(0) == 0)
    def _():
        xhalo[...] = jnp.zeros_like(xhalo)
        phalo[...] = jnp.zeros_like(phalo)
        hcar[...] = jnp.zeros_like(hcar)
        abuf[:, 0:pad, :] = jnp.ones((abuf.shape[0], pad, D_LRU), F32)
        bbuf[:, 0:pad, :] = jnp.zeros((bbuf.shape[0], pad, D_LRU), F32)

    for r in range(lru_ref.shape[0]):
        _lru_sc_chain(lru_ref.at[r], sc_ref.at[r], *params, ylru_ref.at[r], ysc_ref.at[r],
                      *(buf.at[r] for buf in scratch))


def _lru_sc_chain(lru_ref, sc_ref, cw_ref, cb_ref, wgate_ref, bgate_ref, lam_ref, scw_ref,
                  glru_ref, gsc_ref, avg_ref, shift_ref,
                  ylru_ref, ysc_ref, xhalo, phalo, abuf, bbuf, hcar):
    tc = lru_ref.shape[0]
    pad = tc // 2

    def shifted(x_b, d):
        return jnp.dot(shift_ref[d - 1], x_b, preferred_element_type=F32)

    def with_halo(y, corr):
        return jnp.concatenate([y[0:SUBLANES, :] + corr, y[SUBLANES:, :]], axis=0)

    x_b = lru_ref[:, 0:D_LRU]
    conv = cb_ref[...] + cw_ref[LRU_CONV - 1:LRU_CONV, :] * x_b.astype(F32)
    corr = jnp.zeros((SUBLANES, D_LRU), F32)
    for d in range(1, LRU_CONV):
        w = cw_ref[LRU_CONV - 1 - d:LRU_CONV - d, :]
        conv = conv + w * shifted(x_b, d)
        corr = corr + w * xhalo[pl.ds(SUBLANES - d, SUBLANES), :]
    conv = with_halo(conv, corr)
    xhalo[0:SUBLANES, :] = x_b[tc - SUBLANES:tc, :].astype(F32)

    gates = jnp.dot(conv.astype(BF16), wgate_ref[...], preferred_element_type=F32) + bgate_ref[...]
    gate_r = _sigmoid(gates[:, 0:D_LRU])
    gate_i = _sigmoid(gates[:, D_LRU:])
    lam = lam_ref[...]
    log_sig_lam = jnp.minimum(lam, 0.0) - jnp.log1p(jnp.exp(-jnp.abs(lam)))
    log_a = (LRU_C * gate_r) * log_sig_lam
    a = jnp.exp(log_a)
    v = 1.0 - a * a
    u = jnp.where(v > 0.0, v * lax.rsqrt(v), 0.0) * (gate_i * conv)

    s = 1
    while s < tc:
        abuf[pad:pad + tc, :] = a
        bbuf[pad:pad + tc, :] = u
        a_sh = abuf[pl.ds(pad - s, tc), :]
        u_sh = bbuf[pl.ds(pad - s, tc), :]
        u = a * u_sh + u
        a = a * a_sh
        s *= 2
    h = a * hcar[0:1, :] + u
    hcar[...] = jnp.broadcast_to(h[tc - 1:tc, :], hcar.shape)

    ylru_ref[...] = _head_norm(h * jax.nn.gelu(lru_ref[:, D_LRU:].astype(F32)), glru_ref[...],
                               avg_ref[...])

    c_b = sc_ref[:, D_SC:2 * D_SC]
    s_b = sc_ref[:, 2 * D_SC:]
    p = c_b.astype(F32) * s_b.astype(F32)
    acc = scw_ref[SC_CONV - 1:SC_CONV, :] * p
    corr = jnp.zeros((SUBLANES, D_SC), F32)
    for d in range(1, SC_CONV):
        w = scw_ref[SC_CONV - 1 - d:SC_CONV - d, :]
        acc = acc + w * (shifted(c_b, d) * shifted(s_b, d))
        corr = corr + w * phalo[pl.ds(SUBLANES - d, SUBLANES), :]
    acc = with_halo(acc, corr)
    phalo[0:SUBLANES, :] = p[tc - SUBLANES:tc, :]
    ysc_ref[...] = _head_norm(sc_ref[:, 0:D_SC].astype(F32) * acc, gsc_ref[...],
                              avg_ref[0:D_SC, 0:D_SC])


def _lru_sc(lru, sc, cw, cb, wgate, bgate, lam, scw, g_lru, g_sc, batch, seq):
    tc = LRU_CHUNK
    nt = seq // tc
    avg = _head_avg_matrix(D_LRU)
    t_out = lax.broadcasted_iota(jnp.int32, (LRU_CONV - 1, tc, tc), 1)
    t_in = lax.broadcasted_iota(jnp.int32, (LRU_CONV - 1, tc, tc), 2)
    delay = lax.broadcasted_iota(jnp.int32, (LRU_CONV - 1, tc, tc), 0) + 1
    shift = jnp.where(t_in == t_out - delay, 1.0, 0.0).astype(BF16)
    row = lambda c: pl.BlockSpec((batch, tc, c), lambda t: (0, t, 0))
    full = lambda a: pl.BlockSpec(a.shape, lambda t: (0,) * a.ndim)
    per_row = lambda rows, c: pltpu.VMEM((batch, rows, c), F32)
    y_lru, y_sc = pl.pallas_call(
        _lru_sc_kernel,
        grid=(nt,),
        in_specs=[row(2 * D_LRU), row(3 * D_SC), full(cw), full(cb), full(wgate), full(bgate),
                  full(lam), full(scw), full(g_lru), full(g_sc), full(avg), full(shift)],
        out_specs=[row(D_LRU), row(D_SC)],
        out_shape=[jax.ShapeDtypeStruct((batch, seq, D_LRU), BF16),
                   jax.ShapeDtypeStruct((batch, seq, D_SC), BF16)],
        scratch_shapes=[per_row(2 * SUBLANES, D_LRU), per_row(2 * SUBLANES, D_SC),
                        per_row(tc + tc // 2, D_LRU), per_row(tc + tc // 2, D_LRU),
                        per_row(SUBLANES, D_LRU)],
        compiler_params=_params("arbitrary"),
        name="lru_shortconv",
    )(lru.reshape(batch, seq, -1), sc.reshape(batch, seq, -1), cw, cb, wgate, bgate, lam, scw,
      g_lru, g_sc, avg, shift)
    return y_lru.reshape(batch * seq, D_LRU), y_sc.reshape(batch * seq, D_SC)


def _sb_attn_kernel(q_ref, k_ref, v_ref, cm_ref, g_ref, avg_ref, o_ref, o_acc, c_acc):
    blk = cm_ref.shape[0]
    nqb = q_ref.shape[0] // blk
    npair = q_ref.shape[1] // LANES
    first = pl.program_id(1) * nqb
    rows = lambda b: slice(b * blk, (b + 1) * blk)
    head0 = lax.broadcasted_iota(jnp.int32, (1, LANES), 1) < HEAD_DIM
    lanes = lambda p: slice(p * LANES, (p + 1) * LANES)

    def stacked_q(b, p):
        q = q_ref[rows(b), lanes(p)] * jnp.asarray(HEAD_DIM ** -0.5, BF16)
        zero = jnp.zeros_like(q)
        return jnp.concatenate([jnp.where(head0, q, zero), jnp.where(head0, zero, q)], axis=0)

    def step(p, qq, j, o, carry, strict_mask=None, valid=None):
        start = pl.multiple_of(j * blk, blk)
        kt = k_ref[pl.ds(start, blk), lanes(p)]
        vt = v_ref[pl.ds(start, blk), lanes(p)]
        z = lax.dot_general(qq, kt, (((1,), (1,)), ((), ())), preferred_element_type=F32)
        neg_abs = lax.bitcast_convert_type(
            lax.bitcast_convert_type(z, jnp.uint32) | jnp.uint32(0x80000000), F32)
        soft = jnp.log(1.0 + jnp.exp(neg_abs))
        log_beta = jnp.minimum(z, 0.0) - soft
        log_keep = log_beta - z
        keep = strict_mask if valid is None else valid
        if keep is not None:
            log_keep = jnp.where(keep, log_keep, 0.0)
        cs = jnp.dot(log_keep.astype(BF16), cm_ref[...], preferred_element_type=F32)
        later = cs[:, 0:blk] + jnp.concatenate([carry] * (blk // LANES), axis=1)
        w = jnp.exp(log_beta + later)
        if keep is not None:
            w = jnp.where(keep, w, 0.0)
        o = o + jnp.dot(w.astype(BF16), vt, preferred_element_type=F32)
        return o, carry + cs[:, blk:]

    row = lax.broadcasted_iota(jnp.int32, (2 * blk, blk), 0)
    col = lax.broadcasted_iota(jnp.int32, (2 * blk, blk), 1)
    strict = col < jnp.where(row >= blk, row - blk, row)
    zeros = jnp.zeros((2 * blk, LANES), F32)
    qqs = [[stacked_q(b, p) for p in range(npair)] for b in range(nqb)]
    lives = []
    for b in range(nqb):
        i = first + b
        live = None
        for p in range(npair):
            o, carry = step(p, qqs[b][p], i, zeros, zeros, strict_mask=strict)
            o, carry = step(p, qqs[b][p], jnp.maximum(i - 1, 0), o, carry, valid=i > 0)
            o_acc[b * npair + p] = o
            c_acc[b * npair + p] = carry
            top = jnp.max(carry)
            live = top if live is None else jnp.maximum(live, top)
        lives.append(live)

    for b in range(nqb):
        i = first + b

        def cond(state, i=i):
            n, live = state
            return jnp.logical_and(n < i, live > F32_EXP_ZERO)

        def body(state, i=i, b=b):
            n, _ = state
            live = None
            for p in range(npair):
                c = b * npair + p
                o, carry = step(p, qqs[b][p], i - 1 - n, o_acc[c], c_acc[c])
                o_acc[c] = o
                c_acc[c] = carry
                top = jnp.max(carry)
                live = top if live is None else jnp.maximum(live, top)
            return n + 1, live

        lax.while_loop(cond, body, (jnp.int32(1), lives[b]))

    for b in range(nqb):
        for p in range(npair):
            o = o_acc[b * npair + p]
            o_ref[rows(b), lanes(p)] = _head_norm(jnp.where(head0, o[0:blk, :], o[blk:, :]),
                                                  g_ref[:, lanes(p)], avg_ref[...])


def _cumsum_matrix(blk):
    j = lax.broadcasted_iota(jnp.int32, (blk, blk + LANES), 0)
    s = lax.broadcasted_iota(jnp.int32, (blk, blk + LANES), 1)
    return jnp.where((j > s) | (s >= blk), 1.0, 0.0).astype(BF16)


def _sb_attention(qkv, g_sb, batch, seq):
    blk = ATT_BLOCK
    rows = ATT_QBLOCKS * blk
    nq = seq // rows
    npair = D_SB // LANES
    cm = _cumsum_matrix(blk)
    avg = _head_avg_matrix(LANES)
    return pl.pallas_call(
        _sb_attn_kernel,
        grid=(batch, nq),
        in_specs=[pl.BlockSpec((rows, D_SB), lambda b, i: (b * nq + i, 0)),
                  pl.BlockSpec((seq, D_SB), lambda b, i: (b, 1)),
                  pl.BlockSpec((seq, D_SB), lambda b, i: (b, 2)),
                  pl.BlockSpec(cm.shape, lambda b, i: (0, 0)),
                  pl.BlockSpec(g_sb.shape, lambda b, i: (0, 0)),
                  pl.BlockSpec(avg.shape, lambda b, i: (0, 0))],
        out_specs=pl.BlockSpec((rows, D_SB), lambda b, i: (b * nq + i, 0)),
        out_shape=jax.ShapeDtypeStruct((batch * seq, D_SB), BF16),
        scratch_shapes=[pltpu.VMEM((ATT_QBLOCKS * npair, 2 * blk, LANES), F32),
                        pltpu.VMEM((ATT_QBLOCKS * npair, 2 * blk, LANES), F32)],
        compiler_params=_params("parallel", "arbitrary"),
        name="stickbreak_attn",
    )(qkv, qkv, qkv, cm, g_sb, avg)


def _mix_residual(ylru_ref, ysb_ref, ysc_ref, w_ref, h_ref, ycat_ref):
    lo = 0
    for y_ref in (ylru_ref, ysb_ref, ysc_ref):
        c = y_ref.shape[1]
        ycat_ref[:, lo:lo + c] = y_ref[...]
        lo += c
    return h_ref[...] + jnp.dot(ycat_ref[...], w_ref[...], preferred_element_type=F32)


def _mix_specs(w_out_all, layer, rows):
    row = lambda cdim: pl.BlockSpec((rows, cdim), lambda i: (i, 0))
    return [row(D_LRU), row(D_SB), row(D_SC),
            pl.BlockSpec((None,) + w_out_all.shape[1:], lambda i: (layer, 0, 0)), row(D_MODEL)]


def _swiglu_partial(xn, wg_ref, wu_ref, wd_ref, sub):
    width = wg_ref.shape[1]
    out = None
    for lo in range(0, width, sub):
        hi = min(lo + sub, width)
        a = jnp.dot(xn, wg_ref[:, lo:hi], preferred_element_type=F32)
        b = jnp.dot(xn, wu_ref[:, lo:hi], preferred_element_type=F32)
        hid = (a * _sigmoid(a) * b).astype(BF16)
        part = jnp.dot(hid, wd_ref[lo:hi, :], preferred_element_type=F32)
        out = part if out is None else out + part
    return out


def _ffn_kernel(ylru_ref, ysb_ref, ysc_ref, wout_ref, h_ref, g_ref, wg_ref, wu_ref, wd_ref, o_ref,
                ycat_ref):
    h = _mix_residual(ylru_ref, ysb_ref, ysc_ref, wout_ref, h_ref, ycat_ref)
    xn = _rms_norm_rows(h, g_ref[...]).astype(BF16)
    o_ref[...] = h + _swiglu_partial(xn, wg_ref, wu_ref, wd_ref, FF_SUB)


def _outproj_ffn(ys, w_out, mix_layer, h, g, wg, wu, wd, layer):
    t = h.shape[0]
    full = lambda a: pl.BlockSpec(a.shape, lambda i: (0, 0))
    stacked = lambda a: pl.BlockSpec((None,) + a.shape[1:], lambda i: (layer, 0, 0))
    return pl.pallas_call(
        _ffn_kernel,
        grid=(t // ROW_TILE,),
        in_specs=_mix_specs(w_out, mix_layer, ROW_TILE) + [full(g), stacked(wg), stacked(wu),
                                                            stacked(wd)],
        out_specs=pl.BlockSpec((ROW_TILE, D_MODEL), lambda i: (i, 0)),
        out_shape=jax.ShapeDtypeStruct((t, D_MODEL), F32),
        scratch_shapes=[pltpu.VMEM((ROW_TILE, D_MIX), BF16)],
        compiler_params=_params("parallel"),
        name="outproj_norm_swiglu",
    )(*ys, w_out, h, g, wg, wu, wd)


def _top2_gates(logits):
    lane = lax.broadcasted_iota(jnp.int32, logits.shape, 1)
    m1 = jnp.max(logits, axis=-1, keepdims=True)
    i1 = jnp.min(jnp.where(logits == m1, lane, N_EXPERTS), axis=-1, keepdims=True)
    first = lane == i1
    rest = jnp.where(first, -jnp.inf, logits)
    m2 = jnp.max(rest, axis=-1, keepdims=True)
    i2 = jnp.min(jnp.where(rest == m2, lane, N_EXPERTS), axis=-1, keepdims=True)
    e2 = jnp.exp(m2 - m1)
    w1 = 1.0 / (1.0 + e2)
    return jnp.where(first, w1, jnp.where(lane == i2, e2 * w1, 0.0))


def _router_logits(xn_f32, wr_hi, wr_lo):
    x_hi = xn_f32.astype(BF16)
    x_lo = (xn_f32 - x_hi.astype(F32)).astype(BF16)
    both = jnp.dot(x_hi, jnp.concatenate([wr_hi, wr_lo], axis=1), preferred_element_type=F32)
    return (both[:, 0:N_EXPERTS] + both[:, N_EXPERTS:]
            + jnp.dot(x_lo, wr_hi, preferred_element_type=F32))


def _router_kernel(ylru_ref, ysb_ref, ysc_ref, wout_ref, h_ref, g_ref, wrh_ref, wrl_ref, ltri_ref,
                   hmid_ref, xn_ref, gate_ref, rank_ref, cum_ref, base_ref, ycat_ref, *,
                   subs_per_tile):
    @pl.when(pl.program_id(0) % subs_per_tile == 0)
    def _():
        base_ref[...] = jnp.zeros_like(base_ref)

    h = _mix_residual(ylru_ref, ysb_ref, ysc_ref, wout_ref, h_ref, ycat_ref)
    hmid_ref[...] = h
    xn = _rms_norm_rows(h, g_ref[...])
    xn_ref[...] = xn.astype(BF16)
    gates = _top2_gates(_router_logits(xn, wrh_ref[...], wrl_ref[...]))
    gate_ref[...] = gates
    sel = gates > 0.0
    picked = jnp.where(sel, 1.0, 0.0).astype(BF16)
    base = base_ref[...]
    for s in range(picked.shape[0] // MOE_SUB):
        rows = slice(s * MOE_SUB, (s + 1) * MOE_SUB)
        count = base + jnp.dot(ltri_ref[...], picked[rows, :], preferred_element_type=F32)
        rank_ref[rows, :] = jnp.where(sel[rows, :], count - 1.0, -1.0).astype(jnp.int32)
        base = count[MOE_SUB - 1:MOE_SUB, :]
        cum_ref[s] = base.astype(jnp.int32)
    base_ref[...] = base


def _outproj_router(ys, w_out, mix_layer, h, g, wr, moe_tile):
    t = h.shape[0]
    wr_hi = wr.astype(BF16)
    wr_lo = (wr - wr_hi.astype(F32)).astype(BF16)
    rt = min(ROUTER_TILE, moe_tile)
    assert moe_tile % rt == 0 and rt % MOE_SUB == 0
    r = lax.broadcasted_iota(jnp.int32, (MOE_SUB, MOE_SUB), 0)
    c = lax.broadcasted_iota(jnp.int32, (MOE_SUB, MOE_SUB), 1)
    ltri = jnp.where(c <= r, 1.0, 0.0).astype(BF16)
    nsteps = t // rt
    subs = rt // MOE_SUB
    full = lambda a: pl.BlockSpec(a.shape, lambda i: (0, 0))
    row = lambda cdim: pl.BlockSpec((rt, cdim), lambda i: (i, 0))
    return pl.pallas_call(
        functools.partial(_router_kernel, subs_per_tile=moe_tile // rt),
        grid=(nsteps,),
        in_specs=_mix_specs(w_out, mix_layer, rt) + [full(g), full(wr_hi), full(wr_lo), full(ltri)],
        out_specs=[row(D_MODEL), row(D_MODEL), row(N_EXPERTS), row(N_EXPERTS),
                   pl.BlockSpec((subs, 1, N_EXPERTS), lambda i: (i, 0, 0))],
        out_shape=[jax.ShapeDtypeStruct((t, D_MODEL), F32),
                   jax.ShapeDtypeStruct((t, D_MODEL), BF16),
                   jax.ShapeDtypeStruct((t, N_EXPERTS), F32),
                   jax.ShapeDtypeStruct((t, N_EXPERTS), jnp.int32),
                   jax.ShapeDtypeStruct((t // MOE_SUB, 1, N_EXPERTS), jnp.int32)],
        scratch_shapes=[pltpu.VMEM((1, N_EXPERTS), F32), pltpu.VMEM((rt, D_MIX), BF16)],
        compiler_params=_params("arbitrary"),
        name="outproj_norm_router",
    )(*ys, w_out, h, g, wr_hi, wr_lo, ltri)


def _moe_kernel(cum_ref, x_ref, rank_ref, gate_ref, wg_ref, wu_ref, wd_ref, o_ref,
                xc_ref, yc_ref, tmp_ref):
    i, e, f = pl.program_id(0), pl.program_id(1), pl.program_id(2)
    tile = x_ref.shape[0]
    nsub = tile // MOE_SUB
    base = (i * N_EXPERTS + e) * (nsub + 1)
    nchunk = (cum_ref[base + nsub] + MOE_CHUNK - 1) // MOE_CHUNK

    @pl.when((e == 0) & (f == 0))
    def _():
        o_ref[...] = jnp.zeros_like(o_ref)

    def overlaps(c, s):
        return (cum_ref[base + s] < (c + 1) * MOE_CHUNK) & (cum_ref[base + s + 1] > c * MOE_CHUNK)

    def match(c, s):
        rows = c * MOE_CHUNK + lax.broadcasted_iota(jnp.int32, (MOE_CHUNK, MOE_SUB), 0)
        return rank_ref[:, s * MOE_SUB:(s + 1) * MOE_SUB] == rows

    @pl.when(f == 0)
    def _():
        def gather(c, _):
            tmp_ref[...] = jnp.zeros_like(tmp_ref)
            for s in range(nsub):
                @pl.when(overlaps(c, s))
                def _():
                    p = jnp.where(match(c, s), 1.0, 0.0).astype(BF16)
                    tmp_ref[...] += jnp.dot(p, x_ref[s * MOE_SUB:(s + 1) * MOE_SUB, :],
                                            preferred_element_type=F32)
            xc_ref[pl.ds(pl.multiple_of(c * MOE_CHUNK, MOE_CHUNK), MOE_CHUNK), :] = (
                tmp_ref[...].astype(BF16))
            return 0
        lax.fori_loop(0, nchunk, gather, 0)

    def ffn_rows(start, m):
        rows = pl.ds(start, m)
        y = _swiglu_partial(xc_ref[rows, :], wg_ref, wu_ref, wd_ref, MOE_FF_SUB)
        tail = -m % MOE_CHUNK

        @pl.when(f == 0)
        def _():
            yc_ref[rows, :] = y
            if tail:
                yc_ref[pl.ds(start + m, tail), :] = jnp.zeros((tail, D_MODEL), F32)

        @pl.when(f != 0)
        def _():
            yc_ref[rows, :] += y

    count = cum_ref[base + nsub]
    lo = 0
    for m in [v for v in MOE_GROUP_ROWS if v <= xc_ref.shape[0]]:
        @pl.when((count > lo) & (count <= m))
        def _():
            ffn_rows(0, m)
        lo = m

    @pl.when(count > lo)
    def _():
        ffn_rows(0, lo)

        def extra(c, _):
            ffn_rows(pl.multiple_of(c * MOE_CHUNK, MOE_CHUNK), MOE_CHUNK)
            return 0
        lax.fori_loop(lo // MOE_CHUNK, nchunk, extra, 0)

    @pl.when(f == pl.num_programs(2) - 1)
    def _():
        def scatter(c, _):
            y = yc_ref[pl.ds(pl.multiple_of(c * MOE_CHUNK, MOE_CHUNK), MOE_CHUNK), :].astype(BF16)
            for s in range(nsub):
                @pl.when(overlaps(c, s))
                def _():
                    gate = gate_ref[:, s * MOE_SUB:(s + 1) * MOE_SUB]
                    pg = jnp.where(match(c, s), gate, 0.0).astype(BF16)
                    upd = lax.dot_general(pg, y, (((0,), (0,)), ((), ())),
                                          preferred_element_type=F32)
                    rows = slice(s * MOE_SUB, (s + 1) * MOE_SUB)
                    o_ref[rows, :] = (o_ref[rows, :].astype(F32) + upd).astype(BF16)
            return 0
        lax.fori_loop(0, nchunk, scatter, 0)


def _moe(xn, gates, rank, cum_end, wg, wu, wd, layer, moe_tile):
    t = xn.shape[0]
    nf = D_FF // FF_TILE
    ntile = t // moe_tile
    nsub = moe_tile // MOE_SUB
    capacity = -(-moe_tile // MOE_CHUNK) * MOE_CHUNK
    to_rows = lambda a: a.reshape(ntile, moe_tile, N_EXPERTS).transpose(0, 2, 1).reshape(
        ntile, N_EXPERTS, 1, moe_tile)
    cum = cum_end.reshape(ntile, nsub, N_EXPERTS).transpose(0, 2, 1)
    cum = jnp.concatenate([jnp.zeros((ntile, N_EXPERTS, 1), jnp.int32), cum], axis=-1).reshape(-1)
    meta = pl.BlockSpec((None, None, 1, moe_tile), lambda i, e, f, cum: (i, e, 0, 0))
    grid_spec = pltpu.PrefetchScalarGridSpec(
        num_scalar_prefetch=1,
        grid=(ntile, N_EXPERTS, nf),
        in_specs=[pl.BlockSpec((moe_tile, D_MODEL), lambda i, e, f, cum: (i, 0)),
                  meta, meta,
                  pl.BlockSpec((None, None, D_MODEL, FF_TILE),
                               lambda i, e, f, cum: (layer, e, 0, f)),
                  pl.BlockSpec((None, None, D_MODEL, FF_TILE),
                               lambda i, e, f, cum: (layer, e, 0, f)),
                  pl.BlockSpec((None, None, FF_TILE, D_MODEL),
                               lambda i, e, f, cum: (layer, e, f, 0))],
        out_specs=pl.BlockSpec((moe_tile, D_MODEL), lambda i, e, f, cum: (i, 0)),
        scratch_shapes=[pltpu.VMEM((capacity, D_MODEL), BF16),
                        pltpu.VMEM((capacity, D_MODEL), F32),
                        pltpu.VMEM((MOE_CHUNK, D_MODEL), F32)])
    return pl.pallas_call(
        _moe_kernel,
        grid_spec=grid_spec,
        out_shape=jax.ShapeDtypeStruct((t, D_MODEL), BF16),
        compiler_params=_params("parallel", "arbitrary", "arbitrary"),
        name="moe_swiglu",
    )(cum, xn, to_rows(rank), to_rows(gates), wg, wu, wd)


def _final_norm_kernel(h_ref, y_ref, g_ref, o_ref):
    o_ref[...] = _rms_norm_rows(h_ref[...] + y_ref[...].astype(F32), g_ref[...])


def _final_norm(h, y, g):
    t = h.shape[0]
    row = pl.BlockSpec((ROW_TILE, D_MODEL), lambda i: (i, 0))
    return pl.pallas_call(
        _final_norm_kernel,
        grid=(t // ROW_TILE,),
        in_specs=[row, row, pl.BlockSpec(g.shape, lambda i: (0, 0))],
        out_specs=row,
        out_shape=jax.ShapeDtypeStruct((t, D_MODEL), F32),
        compiler_params=_params("parallel"),
        name="final_norm",
    )(h, y, g)


def _block_diag(w):
    n = w.shape[0]
    eye = jnp.eye(n, dtype=w.dtype)
    return jnp.einsum("hij,hg->higj", w, eye).reshape(n * HEAD_DIM, n * HEAD_DIM)


def kernel(x, mix_norm_g, w_in, lru_conv_w, lru_conv_b, lru_wa, lru_ba, lru_wx, lru_bx, lru_lam,
           sc_conv_w, mix_out_g, w_out, ffn_norm_g, dense_wg, dense_wu, dense_wd,
           router_w, moe_wg, moe_wu, moe_wd, final_norm_g):
    batch, seq, _ = x.shape
    depth = w_in.shape[0]
    assert seq % (ATT_BLOCK * ATT_QBLOCKS) == 0 and seq % LRU_CHUNK == 0
    assert (batch * seq) % ROW_TILE == 0
    h = x.reshape(batch * seq, D_MODEL)
    row = lambda v: v.reshape(1, -1)
    moe_tile = min(MOE_TILE, batch * seq)
    assert (batch * seq) % moe_tile == 0 and moe_tile % MOE_SUB == 0
    w_in, w_out, dense_wg, dense_wu, dense_wd, moe_wg, moe_wu, moe_wd = (
        w.astype(BF16) for w in (w_in, w_out, dense_wg, dense_wu, dense_wd, moe_wg, moe_wu, moe_wd))
    pending = None
    for l in range(depth):
        h, lru, qkv, sc = _inproj(h, pending, row(mix_norm_g[l]), w_in, l)
        wgate = jnp.concatenate([_block_diag(lru_wa[l]), _block_diag(lru_wx[l])], axis=1).astype(BF16)
        bgate = jnp.concatenate([lru_ba[l], lru_bx[l]]).reshape(1, -1)
        g_mix = row(mix_out_g[l])
        y_lru, y_sc = _lru_sc(lru, sc, lru_conv_w[l], row(lru_conv_b[l]), wgate, bgate,
                              row(lru_lam[l]), sc_conv_w[l], g_mix[:, 0:D_LRU],
                              g_mix[:, D_LRU + D_SB:], batch, seq)
        y_sb = _sb_attention(qkv, g_mix[:, D_LRU:D_LRU + D_SB], batch, seq)
        ys = (y_lru, y_sb, y_sc)
        j = l // 2
        if l % 2 == 0:
            h = _outproj_ffn(ys, w_out, l, h, row(ffn_norm_g[l]), dense_wg, dense_wu, dense_wd, j)
            pending = None
        else:
            h, xn, gates, rank, cum_end = _outproj_router(ys, w_out, l, h, row(ffn_norm_g[l]),
                                                          router_w[j], moe_tile)
            pending = _moe(xn, gates, rank, cum_end, moe_wg, moe_wu, moe_wd, j, moe_tile)
    if pending is None:
        pending = jnp.zeros(h.shape, BF16)
    return _final_norm(h, pending, row(final_norm_g)).reshape(batch, seq, D_MODEL)
```

```python
import functools

import jax
import jax.numpy as jnp
from jax import lax
from jax.experimental import pallas as pl
from jax.experimental.pallas import tpu as pltpu

F32 = jnp.float32
BF16 = jnp.bfloat16

D_MODEL = 1024
HEAD_DIM = 64
D_LRU = 384
D_SB = 384
D_SC = 256
D_MIX = D_LRU + D_SB + D_SC
LRU_CONV = 4
SC_CONV = 3
LRU_C = 8.0
D_FF = 2816
N_EXPERTS = 8
EPS = 1e-6
F32_EXP_ZERO = -104.0

V7X_VMEM_LIMIT_BYTES = 56 * 1024 * 1024
SUBLANES = 8
LANES = 128

ROW_TILE = 512
INPROJ_TILE = 1024
LRU_CHUNK = 256
ATT_BLOCK = 256
ATT_QBLOCKS = 4
FF_TILE = 1408
MOE_TILE = 2048
MOE_SUB = 512
ROUTER_TILE = 1024
MOE_CHUNK = 256
MOE_GROUP_ROWS = (256, 480, 512, 544, 576, 608, 640, 768)
FF_SUB = 256
MOE_FF_SUB = 704


def _params(*sem):
    return pltpu.CompilerParams(dimension_semantics=sem, vmem_limit_bytes=V7X_VMEM_LIMIT_BYTES)


def _rms_norm_rows(x, g):
    ms = jnp.mean(x * x, axis=-1, keepdims=True)
    return x * lax.rsqrt(ms + EPS) * g


def _sigmoid(x):
    return 1.0 / (1.0 + jnp.exp(-x))


def _head_norm(y, g, avg):
    ms = jnp.dot((y * y).astype(BF16), avg, preferred_element_type=F32)
    return (y * lax.rsqrt(ms + EPS) * g).astype(BF16)


def _head_avg_matrix(width):
    r = lax.broadcasted_iota(jnp.int32, (width, width), 0) // HEAD_DIM
    c = lax.broadcasted_iota(jnp.int32, (width, width), 1) // HEAD_DIM
    return jnp.where(r == c, 1.0 / HEAD_DIM, 0.0).astype(BF16)


def _inproj_kernel(*refs, has_residual):
    if has_residual:
        h_ref, y_ref, g_ref, w_ref, hout_ref, lru_ref, qkv_ref, sc_ref = refs
        h = h_ref[...] + y_ref[...].astype(F32)
        hout_ref[...] = h
    else:
        h_ref, g_ref, w_ref, lru_ref, qkv_ref, sc_ref = refs
        h = h_ref[...]
    xn = _rms_norm_rows(h, g_ref[...]).astype(BF16)
    c0, c1 = 2 * D_LRU, 2 * D_LRU + 3 * D_SB
    lru_ref[...] = jnp.dot(xn, w_ref[:, 0:c0], preferred_element_type=F32).astype(BF16)
    qkv_ref[...] = jnp.dot(xn, w_ref[:, c0:c1], preferred_element_type=F32).astype(BF16)
    sc_ref[...] = jnp.dot(xn, w_ref[:, c1:], preferred_element_type=F32).astype(BF16)


def _inproj(h, y, g, w_all, layer):
    t = h.shape[0]
    d_in = w_all.shape[2]
    row = lambda c: pl.BlockSpec((INPROJ_TILE, c), lambda i: (i, 0))
    acts = [h] if y is None else [h, y]
    proj_shapes = [jax.ShapeDtypeStruct((t, 2 * D_LRU), BF16),
                   jax.ShapeDtypeStruct((t, 3 * D_SB), BF16),
                   jax.ShapeDtypeStruct((t, d_in - 2 * D_LRU - 3 * D_SB), BF16)]
    proj_specs = [row(2 * D_LRU), row(3 * D_SB), row(3 * D_SC)]
    out = pl.pallas_call(
        functools.partial(_inproj_kernel, has_residual=y is not None),
        grid=(t // INPROJ_TILE,),
        in_specs=[row(D_MODEL)] * len(acts) + [
            pl.BlockSpec(g.shape, lambda i: (0, 0)),
            pl.BlockSpec((None, D_MODEL, d_in), lambda i: (layer, 0, 0))],
        out_specs=([] if y is None else [row(D_MODEL)]) + proj_specs,
        out_shape=([] if y is None else [jax.ShapeDtypeStruct((t, D_MODEL), F32)]) + proj_shapes,
        compiler_params=_params("parallel"),
        name="norm_inproj",
    )(*acts, g, w_all)
    return (h, *out) if y is None else tuple(out)


def _lru_sc_kernel(lru_ref, sc_ref, *refs):
    params, (ylru_ref, ysc_ref), scratch = refs[:10], refs[10:12], refs[12:]
    xhalo, phalo, abuf, bbuf, hcar = scratch
    pad = lru_ref.shape[1] // 2

    @pl.when(pl.program_id(0) == 0)
    def _():
        xhalo[...] = jnp.zeros_like(xhalo)
        phalo[...] = jnp.zeros_like(phalo)
        hcar[...] = jnp.zeros_like(hcar)
        abuf[:, 0:pad, :] = jnp.ones((abuf.shape[0], pad, D_LRU), F32)
        bbuf[:, 0:pad, :] = jnp.zeros((bbuf.shape[0], pad, D_LRU), F32)

    for r in range(lru_ref.shape[0]):
        _lru_sc_chain(lru_ref.at[r], sc_ref.at[r], *params, ylru_ref.at[r], ysc_ref.at[r],
                      *(buf.at[r] for buf in scratch))


def _lru_sc_chain(lru_ref, sc_ref, cw_ref, cb_ref, wgate_ref, bgate_ref, lam_ref, scw_ref,
                  glru_ref, gsc_ref, avg_ref, shift_ref,
                  ylru_ref, ysc_ref, xhalo, phalo, abuf, bbuf, hcar):
    tc = lru_ref.shape[0]
    pad = tc // 2

    def shifted(x_b, d):
        return jnp.dot(shift_ref[d - 1], x_b, preferred_element_type=F32)

    def with_halo(y, corr):
        return jnp.concatenate([y[0:SUBLANES, :] + corr, y[SUBLANES:, :]], axis=0)

    x_b = lru_ref[:, 0:D_LRU]
    conv = cb_ref[...] + cw_ref[LRU_CONV - 1:LRU_CONV, :] * x_b.astype(F32)
    corr = jnp.zeros((SUBLANES, D_LRU), F32)
    for d in range(1, LRU_CONV):
        w = cw_ref[LRU_CONV - 1 - d:LRU_CONV - d, :]
        conv = conv + w * shifted(x_b, d)
        corr = corr + w * xhalo[pl.ds(SUBLANES - d, SUBLANES), :]
    conv = with_halo(conv, corr)
    xhalo[0:SUBLANES, :] = x_b[tc - SUBLANES:tc, :].astype(F32)

    gates = jnp.dot(conv.astype(BF16), wgate_ref[...], preferred_element_type=F32) + bgate_ref[...]
    gate_r = _sigmoid(gates[:, 0:D_LRU])
    gate_i = _sigmoid(gates[:, D_LRU:])
    lam = lam_ref[...]
    log_sig_lam = jnp.minimum(lam, 0.0) - jnp.log1p(jnp.exp(-jnp.abs(lam)))
    log_a = (LRU_C * gate_r) * log_sig_lam
    a = jnp.exp(log_a)
    v = 1.0 - a * a
    u = jnp.where(v > 0.0, v * lax.rsqrt(v), 0.0) * (gate_i * conv)

    s = 1
    while s < tc:
        abuf[pad:pad + tc, :] = a
        bbuf[pad:pad + tc, :] = u
        a_sh = abuf[pl.ds(pad - s, tc), :]
        u_sh = bbuf[pl.ds(pad - s, tc), :]
        u = a * u_sh + u
        a = a * a_sh
        s *= 2
    h = a * hcar[0:1, :] + u
    hcar[...] = jnp.broadcast_to(h[tc - 1:tc, :], hcar.shape)

    ylru_ref[...] = _head_norm(h * jax.nn.gelu(lru_ref[:, D_LRU:].astype(F32)), glru_ref[...],
                               avg_ref[...])

    c_b = sc_ref[:, D_SC:2 * D_SC]
    s_b = sc_ref[:, 2 * D_SC:]
    p = c_b.astype(F32) * s_b.astype(F32)
    acc = scw_ref[SC_CONV - 1:SC_CONV, :] * p
    corr = jnp.zeros((SUBLANES, D_SC), F32)
    for d in range(1, SC_CONV):
        w = scw_ref[SC_CONV - 1 - d:SC_CONV - d, :]
        acc = acc + w * (shifted(c_b, d) * shifted(s_b, d))
        corr = corr + w * phalo[pl.ds(SUBLANES - d, SUBLANES), :]
    acc = with_halo(acc, corr)
    phalo[0:SUBLANES, :] = p[tc - SUBLANES:tc, :]
    ysc_ref[...] = _head_norm(sc_ref[:, 0:D_SC].astype(F32) * acc, gsc_ref[...],
                              avg_ref[0:D_SC, 0:D_SC])


def _lru_sc(lru, sc, cw, cb, wgate, bgate, lam, scw, g_lru, g_sc, batch, seq):
    tc = LRU_CHUNK
    nt = seq // tc
    avg = _head_avg_matrix(D_LRU)
    t_out = lax.broadcasted_iota(jnp.int32, (LRU_CONV - 1, tc, tc), 1)
    t_in = lax.broadcasted_iota(jnp.int32, (LRU_CONV - 1, tc, tc), 2)
    delay = lax.broadcasted_iota(jnp.int32, (LRU_CONV - 1, tc, tc), 0) + 1
    shift = jnp.where(t_in == t_out - delay, 1.0, 0.0).astype(BF16)
    row = lambda c: pl.BlockSpec((batch, tc, c), lambda t: (0, t, 0))
    full = lambda a: pl.BlockSpec(a.shape, lambda t: (0,) * a.ndim)
    per_row = lambda rows, c: pltpu.VMEM((batch, rows, c), F32)
    y_lru, y_sc = pl.pallas_call(
        _lru_sc_kernel,
        grid=(nt,),
        in_specs=[row(2 * D_LRU), row(3 * D_SC), full(cw), full(cb), full(wgate), full(bgate),
                  full(lam), full(scw), full(g_lru), full(g_sc), full(avg), full(shift)],
        out_specs=[row(D_LRU), row(D_SC)],
        out_shape=[jax.ShapeDtypeStruct((batch, seq, D_LRU), BF16),
                   jax.ShapeDtypeStruct((batch, seq, D_SC), BF16)],
        scratch_shapes=[per_row(2 * SUBLANES, D_LRU), per_row(2 * SUBLANES, D_SC),
                        per_row(tc + tc // 2, D_LRU), per_row(tc + tc // 2, D_LRU),
                        per_row(SUBLANES, D_LRU)],
        compiler_params=_params("arbitrary"),
        name="lru_shortconv",
    )(lru.reshape(batch, seq, -1), sc.reshape(batch, seq, -1), cw, cb, wgate, bgate, lam, scw,
      g_lru, g_sc, avg, shift)
    return y_lru.reshape(batch * seq, D_LRU), y_sc.reshape(batch * seq, D_SC)


def _sb_attn_kernel(q_ref, k_ref, v_ref, cm_ref, g_ref, avg_ref, o_ref, o_acc, c_acc):
    blk = cm_ref.shape[0]
    nqb = q_ref.shape[0] // blk
    npair = q_ref.shape[1] // LANES
    first = pl.program_id(1) * nqb
    rows = lambda b: slice(b * blk, (b + 1) * blk)
    head0 = lax.broadcasted_iota(jnp.int32, (1, LANES), 1) < HEAD_DIM
    lanes = lambda p: slice(p * LANES, (p + 1) * LANES)

    def stacked_q(b, p):
        q = q_ref[rows(b), lanes(p)] * jnp.asarray(HEAD_DIM ** -0.5, BF16)
        zero = jnp.zeros_like(q)
        return jnp.concatenate([jnp.where(head0, q, zero), jnp.where(head0, zero, q)], axis=0)

    def step(p, qq, j, o, carry, strict_mask=None, valid=None):
        start = pl.multiple_of(j * blk, blk)
        kt = k_ref[pl.ds(start, blk), lanes(p)]
        vt = v_ref[pl.ds(start, blk), lanes(p)]
        z = lax.dot_general(qq, kt, (((1,), (1,)), ((), ())), preferred_element_type=F32)
        soft = jnp.log(1.0 + jnp.exp(-jnp.abs(z)))
        log_beta = jnp.minimum(z, 0.0) - soft
        log_keep = log_beta - z
        keep = strict_mask if valid is None else valid
        if keep is not None:
            log_keep = jnp.where(keep, log_keep, 0.0)
        cs = jnp.dot(log_keep.astype(BF16), cm_ref[...], preferred_element_type=F32)
        later = cs[:, 0:blk] + jnp.concatenate([carry] * (blk // LANES), axis=1)
        w = jnp.exp(log_beta + later)
        if keep is not None:
            w = jnp.where(keep, w, 0.0)
        o = o + jnp.dot(w.astype(BF16), vt, preferred_element_type=F32)
        return o, carry + cs[:, blk:]

    row = lax.broadcasted_iota(jnp.int32, (2 * blk, blk), 0)
    col = lax.broadcasted_iota(jnp.int32, (2 * blk, blk), 1)
    strict = col < jnp.where(row >= blk, row - blk, row)
    zeros = jnp.zeros((2 * blk, LANES), F32)
    qqs = [[stacked_q(b, p) for p in range(npair)] for b in range(nqb)]
    lives = []
    for b in range(nqb):
        i = first + b
        live = None
        for p in range(npair):
            o, carry = step(p, qqs[b][p], i, zeros, zeros, strict_mask=strict)
            o, carry = step(p, qqs[b][p], jnp.maximum(i - 1, 0), o, carry, valid=i > 0)
            o_acc[b * npair + p] = o
            c_acc[b * npair + p] = carry
            top = jnp.max(carry)
            live = top if live is None else jnp.maximum(live, top)
        lives.append(live)

    for b in range(nqb):
        i = first + b

        def cond(state, i=i):
            n, live = state
            return jnp.logical_and(n < i, live > F32_EXP_ZERO)

        def body(state, i=i, b=b):
            n, _ = state
            live = None
            for p in range(npair):
                c = b * npair + p
                o, carry = step(p, qqs[b][p], i - 1 - n, o_acc[c], c_acc[c])
                o_acc[c] = o
                c_acc[c] = carry
                top = jnp.max(carry)
                live = top if live is None else jnp.maximum(live, top)
            return n + 1, live

        lax.while_loop(cond, body, (jnp.int32(1), lives[b]))

    for b in range(nqb):
        for p in range(npair):
            o = o_acc[b * npair + p]
            o_ref[rows(b), lanes(p)] = _head_norm(jnp.where(head0, o[0:blk, :], o[blk:, :]),
                                                  g_ref[:, lanes(p)], avg_ref[...])


def _cumsum_matrix(blk):
    j = lax.broadcasted_iota(jnp.int32, (blk, blk + LANES), 0)
    s = lax.broadcasted_iota(jnp.int32, (blk, blk + LANES), 1)
    return jnp.where((j > s) | (s >= blk), 1.0, 0.0).astype(BF16)


def _sb_attention(qkv, g_sb, batch, seq):
    blk = ATT_BLOCK
    rows = ATT_QBLOCKS * blk
    nq = seq // rows
    npair = D_SB // LANES
    cm = _cumsum_matrix(blk)
    avg = _head_avg_matrix(LANES)
    return pl.pallas_call(
        _sb_attn_kernel,
        grid=(batch, nq),
        in_specs=[pl.BlockSpec((rows, D_SB), lambda b, i: (b * nq + i, 0)),
                  pl.BlockSpec((seq, D_SB), lambda b, i: (b, 1)),
                  pl.BlockSpec((seq, D_SB), lambda b, i: (b, 2)),
                  pl.BlockSpec(cm.shape, lambda b, i: (0, 0)),
                  pl.BlockSpec(g_sb.shape, lambda b, i: (0, 0)),
                  pl.BlockSpec(avg.shape, lambda b, i: (0, 0))],
        out_specs=pl.BlockSpec((rows, D_SB), lambda b, i: (b * nq + i, 0)),
        out_shape=jax.ShapeDtypeStruct((batch * seq, D_SB), BF16),
        scratch_shapes=[pltpu.VMEM((ATT_QBLOCKS * npair, 2 * blk, LANES), F32),
                        pltpu.VMEM((ATT_QBLOCKS * npair, 2 * blk, LANES), F32)],
        compiler_params=_params("parallel", "arbitrary"),
        name="stickbreak_attn",
    )(qkv, qkv, qkv, cm, g_sb, avg)


def _mix_residual(ylru_ref, ysb_ref, ysc_ref, w_ref, h_ref, ycat_ref):
    lo = 0
    for y_ref in (ylru_ref, ysb_ref, ysc_ref):
        c = y_ref.shape[1]
        ycat_ref[:, lo:lo + c] = y_ref[...]
        lo += c
    return h_ref[...] + jnp.dot(ycat_ref[...], w_ref[...], preferred_element_type=F32)


def _mix_specs(w_out_all, layer, rows):
    row = lambda cdim: pl.BlockSpec((rows, cdim), lambda i: (i, 0))
    return [row(D_LRU), row(D_SB), row(D_SC),
            pl.BlockSpec((None,) + w_out_all.shape[1:], lambda i: (layer, 0, 0)), row(D_MODEL)]


def _swiglu_partial(xn, wg_ref, wu_ref, wd_ref, sub):
    width = wg_ref.shape[1]
    out = None
    for lo in range(0, width, sub):
        hi = min(lo + sub, width)
        a = jnp.dot(xn, wg_ref[:, lo:hi], preferred_element_type=F32)
        b = jnp.dot(xn, wu_ref[:, lo:hi], preferred_element_type=F32)
        hid = (a * _sigmoid(a) * b).astype(BF16)
        part = jnp.dot(hid, wd_ref[lo:hi, :], preferred_element_type=F32)
        out = part if out is None else out + part
    return out


def _ffn_kernel(ylru_ref, ysb_ref, ysc_ref, wout_ref, h_ref, g_ref, wg_ref, wu_ref, wd_ref, o_ref,
                ycat_ref):
    h = _mix_residual(ylru_ref, ysb_ref, ysc_ref, wout_ref, h_ref, ycat_ref)
    xn = _rms_norm_rows(h, g_ref[...]).astype(BF16)
    o_ref[...] = h + _swiglu_partial(xn, wg_ref, wu_ref, wd_ref, FF_SUB)


def _outproj_ffn(ys, w_out, mix_layer, h, g, wg, wu, wd, layer):
    t = h.shape[0]
    full = lambda a: pl.BlockSpec(a.shape, lambda i: (0, 0))
    stacked = lambda a: pl.BlockSpec((None,) + a.shape[1:], lambda i: (layer, 0, 0))
    return pl.pallas_call(
        _ffn_kernel,
        grid=(t // ROW_TILE,),
        in_specs=_mix_specs(w_out, mix_layer, ROW_TILE) + [full(g), stacked(wg), stacked(wu),
                                                            stacked(wd)],
        out_specs=pl.BlockSpec((ROW_TILE, D_MODEL), lambda i: (i, 0)),
        out_shape=jax.ShapeDtypeStruct((t, D_MODEL), F32),
        scratch_shapes=[pltpu.VMEM((ROW_TILE, D_MIX), BF16)],
        compiler_params=_params("parallel"),
        name="outproj_norm_swiglu",
    )(*ys, w_out, h, g, wg, wu, wd)


def _top2_gates(logits):
    lane = lax.broadcasted_iota(jnp.int32, logits.shape, 1)
    m1 = jnp.max(logits, axis=-1, keepdims=True)
    i1 = jnp.min(jnp.where(logits == m1, lane, N_EXPERTS), axis=-1, keepdims=True)
    first = lane == i1
    rest = jnp.where(first, -jnp.inf, logits)
    m2 = jnp.max(rest, axis=-1, keepdims=True)
    i2 = jnp.min(jnp.where(rest == m2, lane, N_EXPERTS), axis=-1, keepdims=True)
    e2 = jnp.exp(m2 - m1)
    w1 = 1.0 / (1.0 + e2)
    return jnp.where(first, w1, jnp.where(lane == i2, e2 * w1, 0.0))


def _router_logits(xn_f32, wr_hi, wr_lo):
    x_hi = xn_f32.astype(BF16)
    x_lo = (xn_f32 - x_hi.astype(F32)).astype(BF16)
    both = jnp.dot(x_hi, jnp.concatenate([wr_hi, wr_lo], axis=1), preferred_element_type=F32)
    return (both[:, 0:N_EXPERTS] + both[:, N_EXPERTS:]
            + jnp.dot(x_lo, wr_hi, preferred_element_type=F32))


def _router_kernel(ylru_ref, ysb_ref, ysc_ref, wout_ref, h_ref, g_ref, wrh_ref, wrl_ref, ltri_ref,
                   hmid_ref, xn_ref, gate_ref, rank_ref, cum_ref, base_ref, ycat_ref, *,
                   subs_per_tile):
    @pl.when(pl.program_id(0) % subs_per_tile == 0)
    def _():
        base_ref[...] = jnp.zeros_like(base_ref)

    h = _mix_residual(ylru_ref, ysb_ref, ysc_ref, wout_ref, h_ref, ycat_ref)
    hmid_ref[...] = h
    xn = _rms_norm_rows(h, g_ref[...])
    xn_ref[...] = xn.astype(BF16)
    gates = _top2_gates(_router_logits(xn, wrh_ref[...], wrl_ref[...]))
    gate_ref[...] = gates
    sel = gates > 0.0
    picked = jnp.where(sel, 1.0, 0.0).astype(BF16)
    base = base_ref[...]
    for s in range(picked.shape[0] // MOE_SUB):
        rows = slice(s * MOE_SUB, (s + 1) * MOE_SUB)
        count = base + jnp.dot(ltri_ref[...], picked[rows, :], preferred_element_type=F32)
        rank_ref[rows, :] = jnp.where(sel[rows, :], count - 1.0, -1.0).astype(jnp.int32)
        base = count[MOE_SUB - 1:MOE_SUB, :]
        cum_ref[s] = base.astype(jnp.int32)
    base_ref[...] = base


def _outproj_router(ys, w_out, mix_layer, h, g, wr, moe_tile):
    t = h.shape[0]
    wr_hi = wr.astype(BF16)
    wr_lo = (wr - wr_hi.astype(F32)).astype(BF16)
    rt = min(ROUTER_TILE, moe_tile)
    assert moe_tile % rt == 0 and rt % MOE_SUB == 0
    r = lax.broadcasted_iota(jnp.int32, (MOE_SUB, MOE_SUB), 0)
    c = lax.broadcasted_iota(jnp.int32, (MOE_SUB, MOE_SUB), 1)
    ltri = jnp.where(c <= r, 1.0, 0.0).astype(BF16)
    nsteps = t // rt
    subs = rt // MOE_SUB
    full = lambda a: pl.BlockSpec(a.shape, lambda i: (0, 0))
    row = lambda cdim: pl.BlockSpec((rt, cdim), lambda i: (i, 0))
    return pl.pallas_call(
        functools.partial(_router_kernel, subs_per_tile=moe_tile // rt),
        grid=(nsteps,),
        in_specs=_mix_specs(w_out, mix_layer, rt) + [full(g), full(wr_hi), full(wr_lo), full(ltri)],
        out_specs=[row(D_MODEL), row(D_MODEL), row(N_EXPERTS), row(N_EXPERTS),
                   pl.BlockSpec((subs, 1, N_EXPERTS), lambda i: (i, 0, 0))],
        out_shape=[jax.ShapeDtypeStruct((t, D_MODEL), F32),
                   jax.ShapeDtypeStruct((t, D_MODEL), BF16),
                   jax.ShapeDtypeStruct((t, N_EXPERTS), F32),
                   jax.ShapeDtypeStruct((t, N_EXPERTS), jnp.int32),
                   jax.ShapeDtypeStruct((t // MOE_SUB, 1, N_EXPERTS), jnp.int32)],
        scratch_shapes=[pltpu.VMEM((1, N_EXPERTS), F32), pltpu.VMEM((rt, D_MIX), BF16)],
        compiler_params=_params("arbitrary"),
        name="outproj_norm_router",
    )(*ys, w_out, h, g, wr_hi, wr_lo, ltri)


def _moe_kernel(cum_ref, x_ref, rank_ref, gate_ref, wg_ref, wu_ref, wd_ref, o_ref,
                xc_ref, yc_ref, tmp_ref):
    i, e, f = pl.program_id(0), pl.program_id(1), pl.program_id(2)
    tile = x_ref.shape[0]
    nsub = tile // MOE_SUB
    base = (i * N_EXPERTS + e) * (nsub + 1)
    nchunk = (cum_ref[base + nsub] + MOE_CHUNK - 1) // MOE_CHUNK

    @pl.when((e == 0) & (f == 0))
    def _():
        o_ref[...] = jnp.zeros_like(o_ref)

    def overlaps(c, s):
        return (cum_ref[base + s] < (c + 1) * MOE_CHUNK) & (cum_ref[base + s + 1] > c * MOE_CHUNK)

    def match(c, s):
        rows = c * MOE_CHUNK + lax.broadcasted_iota(jnp.int32, (MOE_CHUNK, MOE_SUB), 0)
        return rank_ref[:, s * MOE_SUB:(s + 1) * MOE_SUB] == rows

    @pl.when(f == 0)
    def _():
        def gather(c, _):
            tmp_ref[...] = jnp.zeros_like(tmp_ref)
            for s in range(nsub):
                @pl.when(overlaps(c, s))
                def _():
                    p = jnp.where(match(c, s), 1.0, 0.0).astype(BF16)
                    tmp_ref[...] += jnp.dot(p, x_ref[s * MOE_SUB:(s + 1) * MOE_SUB, :],
                                            preferred_element_type=F32)
            xc_ref[pl.ds(pl.multiple_of(c * MOE_CHUNK, MOE_CHUNK), MOE_CHUNK), :] = (
                tmp_ref[...].astype(BF16))
            return 0
        lax.fori_loop(0, nchunk, gather, 0)

    def ffn_rows(start, m):
        rows = pl.ds(start, m)
        y = _swiglu_partial(xc_ref[rows, :], wg_ref, wu_ref, wd_ref, MOE_FF_SUB)
        tail = -m % MOE_CHUNK

        @pl.when(f == 0)
        def _():
            yc_ref[rows, :] = y
            if tail:
                yc_ref[pl.ds(start + m, tail), :] = jnp.zeros((tail, D_MODEL), F32)

        @pl.when(f != 0)
        def _():
            yc_ref[rows, :] += y

    count = cum_ref[base + nsub]
    lo = 0
    for m in [v for v in MOE_GROUP_ROWS if v <= xc_ref.shape[0]]:
        @pl.when((count > lo) & (count <= m))
        def _():
            ffn_rows(0, m)
        lo = m

    @pl.when(count > lo)
    def _():
        ffn_rows(0, lo)

        def extra(c, _):
            ffn_rows(pl.multiple_of(c * MOE_CHUNK, MOE_CHUNK), MOE_CHUNK)
            return 0
        lax.fori_loop(lo // MOE_CHUNK, nchunk, extra, 0)

    @pl.when(f == pl.num_programs(2) - 1)
    def _():
        def scatter(c, _):
            y = yc_ref[pl.ds(pl.multiple_of(c * MOE_CHUNK, MOE_CHUNK), MOE_CHUNK), :].astype(BF16)
            for s in range(nsub):
                @pl.when(overlaps(c, s))
                def _():
                    gate = gate_ref[:, s * MOE_SUB:(s + 1) * MOE_SUB]
                    pg = jnp.where(match(c, s), gate, 0.0).astype(BF16)
                    upd = lax.dot_general(pg, y, (((0,), (0,)), ((), ())),
                                          preferred_element_type=F32)
                    rows = slice(s * MOE_SUB, (s + 1) * MOE_SUB)
                    o_ref[rows, :] = (o_ref[rows, :].astype(F32) + upd).astype(BF16)
            return 0
        lax.fori_loop(0, nchunk, scatter, 0)


def _moe(xn, gates, rank, cum_end, wg, wu, wd, layer, moe_tile):
    t = xn.shape[0]
    nf = D_FF // FF_TILE
    ntile = t // moe_tile
    nsub = moe_tile // MOE_SUB
    capacity = -(-moe_tile // MOE_CHUNK) * MOE_CHUNK
    to_rows = lambda a: a.reshape(ntile, moe_tile, N_EXPERTS).transpose(0, 2, 1).reshape(
        ntile, N_EXPERTS, 1, moe_tile)
    cum = cum_end.reshape(ntile, nsub, N_EXPERTS).transpose(0, 2, 1)
    cum = jnp.concatenate([jnp.zeros((ntile, N_EXPERTS, 1), jnp.int32), cum], axis=-1).reshape(-1)
    meta = pl.BlockSpec((None, None, 1, moe_tile), lambda i, e, f, cum: (i, e, 0, 0))
    grid_spec = pltpu.PrefetchScalarGridSpec(
        num_scalar_prefetch=1,
        grid=(ntile, N_EXPERTS, nf),
        in_specs=[pl.BlockSpec((moe_tile, D_MODEL), lambda i, e, f, cum: (i, 0)),
                  meta, meta,
                  pl.BlockSpec((None, None, D_MODEL, FF_TILE),
                               lambda i, e, f, cum: (layer, e, 0, f)),
                  pl.BlockSpec((None, None, D_MODEL, FF_TILE),
                               lambda i, e, f, cum: (layer, e, 0, f)),
                  pl.BlockSpec((None, None, FF_TILE, D_MODEL),
                               lambda i, e, f, cum: (layer, e, f, 0))],
        out_specs=pl.BlockSpec((moe_tile, D_MODEL), lambda i, e, f, cum: (i, 0)),
        scratch_shapes=[pltpu.VMEM((capacity, D_MODEL), BF16),
                        pltpu.VMEM((capacity, D_MODEL), F32),
                        pltpu.VMEM((MOE_CHUNK, D_MODEL), F32)])
    return pl.pallas_call(
        _moe_kernel,
        grid_spec=grid_spec,
        out_shape=jax.ShapeDtypeStruct((t, D_MODEL), BF16),
        compiler_params=_params("parallel", "arbitrary", "arbitrary"),
        name="moe_swiglu",
    )(cum, xn, to_rows(rank), to_rows(gates), wg, wu, wd)


def _final_norm_kernel(h_ref, y_ref, g_ref, o_ref):
    o_ref[...] = _rms_norm_rows(h_ref[...] + y_ref[...].astype(F32), g_ref[...])


def _final_norm(h, y, g):
    t = h.shape[0]
    row = pl.BlockSpec((ROW_TILE, D_MODEL), lambda i: (i, 0))
    return pl.pallas_call(
        _final_norm_kernel,
        grid=(t // ROW_TILE,),
        in_specs=[row, row, pl.BlockSpec(g.shape, lambda i: (0, 0))],
        out_specs=row,
        out_shape=jax.ShapeDtypeStruct((t, D_MODEL), F32),
        compiler_params=_params("parallel"),
        name="final_norm",
    )(h, y, g)


def _block_diag(w):
    n = w.shape[0]
    eye = jnp.eye(n, dtype=w.dtype)
    return jnp.einsum("hij,hg->higj", w, eye).reshape(n * HEAD_DIM, n * HEAD_DIM)


def kernel(x, mix_norm_g, w_in, lru_conv_w, lru_conv_b, lru_wa, lru_ba, lru_wx, lru_bx, lru_lam,
           sc_conv_w, mix_out_g, w_out, ffn_norm_g, dense_wg, dense_wu, dense_wd,
           router_w, moe_wg, moe_wu, moe_wd, final_norm_g):
    batch, seq, _ = x.shape
    depth = w_in.shape[0]
    assert seq % (ATT_BLOCK * ATT_QBLOCKS) == 0 and seq % LRU_CHUNK == 0
    assert (batch * seq) % ROW_TILE == 0 and (batch * seq) % INPROJ_TILE == 0
    h = x.reshape(batch * seq, D_MODEL)
    row = lambda v: v.reshape(1, -1)
    moe_tile = min(MOE_TILE, batch * seq)
    assert (batch * seq) % moe_tile == 0 and moe_tile % MOE_SUB == 0
    w_in, w_out, dense_wg, dense_wu, dense_wd, moe_wg, moe_wu, moe_wd = (
        w.astype(BF16) for w in (w_in, w_out, dense_wg, dense_wu, dense_wd, moe_wg, moe_wu, moe_wd))
    pending = None
    for l in range(depth):
        h, lru, qkv, sc = _inproj(h, pending, row(mix_norm_g[l]), w_in, l)
        wgate = jnp.concatenate([_block_diag(lru_wa[l]), _block_diag(lru_wx[l])], axis=1).astype(BF16)
        bgate = jnp.concatenate([lru_ba[l], lru_bx[l]]).reshape(1, -1)
        g_mix = row(mix_out_g[l])
        y_lru, y_sc = _lru_sc(lru, sc, lru_conv_w[l], row(lru_conv_b[l]), wgate, bgate,
                              row(lru_lam[l]), sc_conv_w[l], g_mix[:, 0:D_LRU],
                              g_mix[:, D_LRU + D_SB:], batch, seq)
        y_sb = _sb_attention(qkv, g_mix[:, D_LRU:D_LRU + D_SB], batch, seq)
        ys = (y_lru, y_sb, y_sc)
        j = l // 2
        if l % 2 == 0:
            h = _outproj_ffn(ys, w_out, l, h, row(ffn_norm_g[l]), dense_wg, dense_wu, dense_wd, j)
            pending = None
        else:
            h, xn, gates, rank, cum_end = _outproj_router(ys, w_out, l, h, row(ffn_norm_g[l]),
                                                          router_w[j], moe_tile)
            pending = _moe(xn, gates, rank, cum_end, moe_wg, moe_wu, moe_wd, j, moe_tile)
    if pending is None:
        pending = jnp.zeros(h.shape, BF16)
    return _final_norm(h, pending, row(final_norm_g)).reshape(batch, seq, D_MODEL)
```

```python
import functools

import jax
import jax.numpy as jnp
from jax import lax
from jax.experimental import pallas as pl
from jax.experimental.pallas import tpu as pltpu

F32 = jnp.float32
BF16 = jnp.bfloat16

D_MODEL = 1024
HEAD_DIM = 64
D_LRU = 384
D_SB = 384
D_SC = 256
D_MIX = D_LRU + D_SB + D_SC
LRU_CONV = 4
SC_CONV = 3
LRU_C = 8.0
D_FF = 2816
N_EXPERTS = 8
EPS = 1e-6
F32_EXP_ZERO = -104.0

V7X_VMEM_LIMIT_BYTES = 56 * 1024 * 1024
SUBLANES = 8
LANES = 128

ROW_TILE = 512
INPROJ_TILE = 1024
LRU_CHUNK = 256
ATT_BLOCK = 256
ATT_QBLOCKS = 4
FF_TILE = 1408
MOE_TILE = 2048
MOE_SUB = 512
ROUTER_TILE = 1024
MOE_CHUNK = 256
MOE_GROUP_ROWS = (256, 512, 576, 640, 768)
FF_SUB = 256
MOE_FF_SUB = 704


def _params(*sem):
    return pltpu.CompilerParams(dimension_semantics=sem, vmem_limit_bytes=V7X_VMEM_LIMIT_BYTES)


def _rms_norm_rows(x, g):
    ms = jnp.mean(x * x, axis=-1, keepdims=True)
    return x * lax.rsqrt(ms + EPS) * g


def _sigmoid(x):
    return 1.0 / (1.0 + jnp.exp(-x))


def _head_norm(y, g, avg):
    ms = jnp.dot((y * y).astype(BF16), avg, preferred_element_type=F32)
    return (y * lax.rsqrt(ms + EPS) * g).astype(BF16)


def _head_avg_matrix(width):
    r = lax.broadcasted_iota(jnp.int32, (width, width), 0) // HEAD_DIM
    c = lax.broadcasted_iota(jnp.int32, (width, width), 1) // HEAD_DIM
    return jnp.where(r == c, 1.0 / HEAD_DIM, 0.0).astype(BF16)


def _inproj_kernel(*refs, has_residual):
    if has_residual:
        h_ref, y_ref, g_ref, w_ref, hout_ref, lru_ref, qkv_ref, sc_ref = refs
        h = h_ref[...] + y_ref[...].astype(F32)
        hout_ref[...] = h
    else:
        h_ref, g_ref, w_ref, lru_ref, qkv_ref, sc_ref = refs
        h = h_ref[...]
    xn = _rms_norm_rows(h, g_ref[...]).astype(BF16)
    c0, c1 = 2 * D_LRU, 2 * D_LRU + 3 * D_SB
    lru_ref[...] = jnp.dot(xn, w_ref[:, 0:c0], preferred_element_type=F32).astype(BF16)
    qkv_ref[...] = jnp.dot(xn, w_ref[:, c0:c1], preferred_element_type=F32).astype(BF16)
    sc_ref[...] = jnp.dot(xn, w_ref[:, c1:], preferred_element_type=F32).astype(BF16)


def _inproj(h, y, g, w_all, layer):
    t = h.shape[0]
    d_in = w_all.shape[2]
    row = lambda c: pl.BlockSpec((INPROJ_TILE, c), lambda i: (i, 0))
    acts = [h] if y is None else [h, y]
    proj_shapes = [jax.ShapeDtypeStruct((t, 2 * D_LRU), BF16),
                   jax.ShapeDtypeStruct((t, 3 * D_SB), BF16),
                   jax.ShapeDtypeStruct((t, d_in - 2 * D_LRU - 3 * D_SB), BF16)]
    proj_specs = [row(2 * D_LRU), row(3 * D_SB), row(3 * D_SC)]
    out = pl.pallas_call(
        functools.partial(_inproj_kernel, has_residual=y is not None),
        grid=(t // INPROJ_TILE,),
        in_specs=[row(D_MODEL)] * len(acts) + [
            pl.BlockSpec(g.shape, lambda i: (0, 0)),
            pl.BlockSpec((None, D_MODEL, d_in), lambda i: (layer, 0, 0))],
        out_specs=([] if y is None else [row(D_MODEL)]) + proj_specs,
        out_shape=([] if y is None else [jax.ShapeDtypeStruct((t, D_MODEL), F32)]) + proj_shapes,
        compiler_params=_params("parallel"),
        name="norm_inproj",
    )(*acts, g, w_all)
    return (h, *out) if y is None else tuple(out)


def _lru_sc_kernel(lru_ref, sc_ref, *refs):
    params, (ylru_ref, ysc_ref), scratch = refs[:10], refs[10:12], refs[12:]
    xhalo, phalo, abuf, bbuf, hcar = scratch
    pad = lru_ref.shape[1] // 2

    @pl.when(pl.program_id(0) == 0)
    def _():
        xhalo[...] = jnp.zeros_like(xhalo)
        phalo[...] = jnp.zeros_like(phalo)
        hcar[...] = jnp.zeros_like(hcar)
        abuf[:, 0:pad, :] = jnp.ones((abuf.shape[0], pad, D_LRU), F32)
        bbuf[:, 0:pad, :] = jnp.zeros((bbuf.shape[0], pad, D_LRU), F32)

    for r in range(lru_ref.shape[0]):
        _lru_sc_chain(lru_ref.at[r], sc_ref.at[r], *params, ylru_ref.at[r], ysc_ref.at[r],
                      *(buf.at[r] for buf in scratch))


def _lru_sc_chain(lru_ref, sc_ref, cw_ref, cb_ref, wgate_ref, bgate_ref, lam_ref, scw_ref,
                  glru_ref, gsc_ref, avg_ref, shift_ref,
                  ylru_ref, ysc_ref, xhalo, phalo, abuf, bbuf, hcar):
    tc = lru_ref.shape[0]
    pad = tc // 2

    def shifted(x_b, d):
        return jnp.dot(shift_ref[d - 1], x_b, preferred_element_type=F32)

    def with_halo(y, corr):
        return jnp.concatenate([y[0:SUBLANES, :] + corr, y[SUBLANES:, :]], axis=0)

    x_b = lru_ref[:, 0:D_LRU]
    conv = cb_ref[...] + cw_ref[LRU_CONV - 1:LRU_CONV, :] * x_b.astype(F32)
    corr = jnp.zeros((SUBLANES, D_LRU), F32)
    for d in range(1, LRU_CONV):
        w = cw_ref[LRU_CONV - 1 - d:LRU_CONV - d, :]
        conv = conv + w * shifted(x_b, d)
        corr = corr + w * xhalo[pl.ds(SUBLANES - d, SUBLANES), :]
    conv = with_halo(conv, corr)
    xhalo[0:SUBLANES, :] = x_b[tc - SUBLANES:tc, :].astype(F32)

    gates = jnp.dot(conv.astype(BF16), wgate_ref[...], preferred_element_type=F32) + bgate_ref[...]
    gate_r = _sigmoid(gates[:, 0:D_LRU])
    gate_i = _sigmoid(gates[:, D_LRU:])
    lam = lam_ref[...]
    log_sig_lam = jnp.minimum(lam, 0.0) - jnp.log1p(jnp.exp(-jnp.abs(lam)))
    log_a = (LRU_C * gate_r) * log_sig_lam
    a = jnp.exp(log_a)
    v = 1.0 - a * a
    u = jnp.where(v > 0.0, v * lax.rsqrt(v), 0.0) * (gate_i * conv)

    s = 1
    while s < tc:
        abuf[pad:pad + tc, :] = a
        bbuf[pad:pad + tc, :] = u
        a_sh = abuf[pl.ds(pad - s, tc), :]
        u_sh = bbuf[pl.ds(pad - s, tc), :]
        u = a * u_sh + u
        a = a * a_sh
        s *= 2
    h = a * hcar[0:1, :] + u
    hcar[...] = jnp.broadcast_to(h[tc - 1:tc, :], hcar.shape)

    ylru_ref[...] = _head_norm(h * jax.nn.gelu(lru_ref[:, D_LRU:].astype(F32)), glru_ref[...],
                               avg_ref[...])

    c_b = sc_ref[:, D_SC:2 * D_SC]
    s_b = sc_ref[:, 2 * D_SC:]
    p = c_b.astype(F32) * s_b.astype(F32)
    acc = scw_ref[SC_CONV - 1:SC_CONV, :] * p
    corr = jnp.zeros((SUBLANES, D_SC), F32)
    for d in range(1, SC_CONV):
        w = scw_ref[SC_CONV - 1 - d:SC_CONV - d, :]
        acc = acc + w * (shifted(c_b, d) * shifted(s_b, d))
        corr = corr + w * phalo[pl.ds(SUBLANES - d, SUBLANES), :]
    acc = with_halo(acc, corr)
    phalo[0:SUBLANES, :] = p[tc - SUBLANES:tc, :]
    ysc_ref[...] = _head_norm(sc_ref[:, 0:D_SC].astype(F32) * acc, gsc_ref[...],
                              avg_ref[0:D_SC, 0:D_SC])


def _lru_sc(lru, sc, cw, cb, wgate, bgate, lam, scw, g_lru, g_sc, batch, seq):
    tc = LRU_CHUNK
    nt = seq // tc
    avg = _head_avg_matrix(D_LRU)
    t_out = lax.broadcasted_iota(jnp.int32, (LRU_CONV - 1, tc, tc), 1)
    t_in = lax.broadcasted_iota(jnp.int32, (LRU_CONV - 1, tc, tc), 2)
    delay = lax.broadcasted_iota(jnp.int32, (LRU_CONV - 1, tc, tc), 0) + 1
    shift = jnp.where(t_in == t_out - delay, 1.0, 0.0).astype(BF16)
    row = lambda c: pl.BlockSpec((batch, tc, c), lambda t: (0, t, 0))
    full = lambda a: pl.BlockSpec(a.shape, lambda t: (0,) * a.ndim)
    per_row = lambda rows, c: pltpu.VMEM((batch, rows, c), F32)
    y_lru, y_sc = pl.pallas_call(
        _lru_sc_kernel,
        grid=(nt,),
        in_specs=[row(2 * D_LRU), row(3 * D_SC), full(cw), full(cb), full(wgate), full(bgate),
                  full(lam), full(scw), full(g_lru), full(g_sc), full(avg), full(shift)],
        out_specs=[row(D_LRU), row(D_SC)],
        out_shape=[jax.ShapeDtypeStruct((batch, seq, D_LRU), BF16),
                   jax.ShapeDtypeStruct((batch, seq, D_SC), BF16)],
        scratch_shapes=[per_row(2 * SUBLANES, D_LRU), per_row(2 * SUBLANES, D_SC),
                        per_row(tc + tc // 2, D_LRU), per_row(tc + tc // 2, D_LRU),
                        per_row(SUBLANES, D_LRU)],
        compiler_params=_params("arbitrary"),
        name="lru_shortconv",
    )(lru.reshape(batch, seq, -1), sc.reshape(batch, seq, -1), cw, cb, wgate, bgate, lam, scw,
      g_lru, g_sc, avg, shift)
    return y_lru.reshape(batch * seq, D_LRU), y_sc.reshape(batch * seq, D_SC)


def _sb_attn_kernel(q_ref, k_ref, v_ref, cm_ref, g_ref, avg_ref, o_ref, o_acc, c_acc):
    blk = cm_ref.shape[0]
    nqb = q_ref.shape[0] // blk
    npair = q_ref.shape[1] // LANES
    first = pl.program_id(1) * nqb
    rows = lambda b: slice(b * blk, (b + 1) * blk)
    head0 = lax.broadcasted_iota(jnp.int32, (1, LANES), 1) < HEAD_DIM
    lanes = lambda p: slice(p * LANES, (p + 1) * LANES)

    def stacked_q(b, p):
        q = q_ref[rows(b), lanes(p)] * jnp.asarray(HEAD_DIM ** -0.5, BF16)
        zero = jnp.zeros_like(q)
        return jnp.concatenate([jnp.where(head0, q, zero), jnp.where(head0, zero, q)], axis=0)

    def step(p, qq, j, o, carry, strict_mask=None, valid=None):
        start = pl.multiple_of(j * blk, blk)
        kt = k_ref[pl.ds(start, blk), lanes(p)]
        vt = v_ref[pl.ds(start, blk), lanes(p)]
        z = lax.dot_general(qq, kt, (((1,), (1,)), ((), ())), preferred_element_type=F32)
        soft = jnp.log(1.0 + jnp.exp(-jnp.abs(z)))
        log_beta = jnp.minimum(z, 0.0) - soft
        log_keep = log_beta - z
        keep = strict_mask if valid is None else valid
        if keep is not None:
            log_keep = jnp.where(keep, log_keep, 0.0)
        cs = jnp.dot(log_keep.astype(BF16), cm_ref[...], preferred_element_type=F32)
        later = cs[:, 0:blk] + jnp.concatenate([carry] * (blk // LANES), axis=1)
        w = jnp.exp(log_beta + later)
        if keep is not None:
            w = jnp.where(keep, w, 0.0)
        o = o + jnp.dot(w.astype(BF16), vt, preferred_element_type=F32)
        return o, carry + cs[:, blk:]

    row = lax.broadcasted_iota(jnp.int32, (2 * blk, blk), 0)
    col = lax.broadcasted_iota(jnp.int32, (2 * blk, blk), 1)
    strict = col < jnp.where(row >= blk, row - blk, row)
    zeros = jnp.zeros((2 * blk, LANES), F32)
    qqs = [[stacked_q(b, p) for p in range(npair)] for b in range(nqb)]
    lives = []
    for b in range(nqb):
        i = first + b
        live = None
        for p in range(npair):
            o, carry = step(p, qqs[b][p], i, zeros, zeros, strict_mask=strict)
            o, carry = step(p, qqs[b][p], jnp.maximum(i - 1, 0), o, carry, valid=i > 0)
            o_acc[b * npair + p] = o
            c_acc[b * npair + p] = carry
            top = jnp.max(carry)
            live = top if live is None else jnp.maximum(live, top)
        lives.append(live)

    for b in range(nqb):
        i = first + b

        def cond(state, i=i):
            n, live = state
            return jnp.logical_and(n < i, live > F32_EXP_ZERO)

        def body(state, i=i, b=b):
            n, _ = state
            live = None
            for p in range(npair):
                c = b * npair + p
                o, carry = step(p, qqs[b][p], i - 1 - n, o_acc[c], c_acc[c])
                o_acc[c] = o
                c_acc[c] = carry
                top = jnp.max(carry)
                live = top if live is None else jnp.maximum(live, top)
            return n + 1, live

        lax.while_loop(cond, body, (jnp.int32(1), lives[b]))

    for b in range(nqb):
        for p in range(npair):
            o = o_acc[b * npair + p]
            o_ref[rows(b), lanes(p)] = _head_norm(jnp.where(head0, o[0:blk, :], o[blk:, :]),
                                                  g_ref[:, lanes(p)], avg_ref[...])


def _cumsum_matrix(blk):
    j = lax.broadcasted_iota(jnp.int32, (blk, blk + LANES), 0)
    s = lax.broadcasted_iota(jnp.int32, (blk, blk + LANES), 1)
    return jnp.where((j > s) | (s >= blk), 1.0, 0.0).astype(BF16)


def _sb_attention(qkv, g_sb, batch, seq):
    blk = ATT_BLOCK
    rows = ATT_QBLOCKS * blk
    nq = seq // rows
    npair = D_SB // LANES
    cm = _cumsum_matrix(blk)
    avg = _head_avg_matrix(LANES)
    return pl.pallas_call(
        _sb_attn_kernel,
        grid=(batch, nq),
        in_specs=[pl.BlockSpec((rows, D_SB), lambda b, i: (b * nq + i, 0)),
                  pl.BlockSpec((seq, D_SB), lambda b, i: (b, 1)),
                  pl.BlockSpec((seq, D_SB), lambda b, i: (b, 2)),
                  pl.BlockSpec(cm.shape, lambda b, i: (0, 0)),
                  pl.BlockSpec(g_sb.shape, lambda b, i: (0, 0)),
                  pl.BlockSpec(avg.shape, lambda b, i: (0, 0))],
        out_specs=pl.BlockSpec((rows, D_SB), lambda b, i: (b * nq + i, 0)),
        out_shape=jax.ShapeDtypeStruct((batch * seq, D_SB), BF16),
        scratch_shapes=[pltpu.VMEM((ATT_QBLOCKS * npair, 2 * blk, LANES), F32),
                        pltpu.VMEM((ATT_QBLOCKS * npair, 2 * blk, LANES), F32)],
        compiler_params=_params("parallel", "arbitrary"),
        name="stickbreak_attn",
    )(qkv, qkv, qkv, cm, g_sb, avg)


def _mix_residual(ylru_ref, ysb_ref, ysc_ref, w_ref, h_ref, ycat_ref):
    lo = 0
    for y_ref in (ylru_ref, ysb_ref, ysc_ref):
        c = y_ref.shape[1]
        ycat_ref[:, lo:lo + c] = y_ref[...]
        lo += c
    return h_ref[...] + jnp.dot(ycat_ref[...], w_ref[...], preferred_element_type=F32)


def _mix_specs(w_out_all, layer, rows):
    row = lambda cdim: pl.BlockSpec((rows, cdim), lambda i: (i, 0))
    return [row(D_LRU), row(D_SB), row(D_SC),
            pl.BlockSpec((None,) + w_out_all.shape[1:], lambda i: (layer, 0, 0)), row(D_MODEL)]


def _swiglu_partial(xn, wg_ref, wu_ref, wd_ref, sub):
    width = wg_ref.shape[1]
    out = None
    for lo in range(0, width, sub):
        hi = min(lo + sub, width)
        a = jnp.dot(xn, wg_ref[:, lo:hi], preferred_element_type=F32)
        b = jnp.dot(xn, wu_ref[:, lo:hi], preferred_element_type=F32)
        hid = (a * _sigmoid(a) * b).astype(BF16)
        part = jnp.dot(hid, wd_ref[lo:hi, :], preferred_element_type=F32)
        out = part if out is None else out + part
    return out


def _ffn_kernel(ylru_ref, ysb_ref, ysc_ref, wout_ref, h_ref, g_ref, wg_ref, wu_ref, wd_ref, o_ref,
                ycat_ref):
    h = _mix_residual(ylru_ref, ysb_ref, ysc_ref, wout_ref, h_ref, ycat_ref)
    xn = _rms_norm_rows(h, g_ref[...]).astype(BF16)
    o_ref[...] = h + _swiglu_partial(xn, wg_ref, wu_ref, wd_ref, FF_SUB)


def _outproj_ffn(ys, w_out, mix_layer, h, g, wg, wu, wd, layer):
    t = h.shape[0]
    full = lambda a: pl.BlockSpec(a.shape, lambda i: (0, 0))
    stacked = lambda a: pl.BlockSpec((None,) + a.shape[1:], lambda i: (layer, 0, 0))
    return pl.pallas_call(
        _ffn_kernel,
        grid=(t // ROW_TILE,),
        in_specs=_mix_specs(w_out, mix_layer, ROW_TILE) + [full(g), stacked(wg), stacked(wu),
                                                            stacked(wd)],
        out_specs=pl.BlockSpec((ROW_TILE, D_MODEL), lambda i: (i, 0)),
        out_shape=jax.ShapeDtypeStruct((t, D_MODEL), F32),
        scratch_shapes=[pltpu.VMEM((ROW_TILE, D_MIX), BF16)],
        compiler_params=_params("parallel"),
        name="outproj_norm_swiglu",
    )(*ys, w_out, h, g, wg, wu, wd)


def _top2_gates(logits):
    lane = lax.broadcasted_iota(jnp.int32, logits.shape, 1)
    m1 = jnp.max(logits, axis=-1, keepdims=True)
    i1 = jnp.min(jnp.where(logits == m1, lane, N_EXPERTS), axis=-1, keepdims=True)
    first = lane == i1
    rest = jnp.where(first, -jnp.inf, logits)
    m2 = jnp.max(rest, axis=-1, keepdims=True)
    i2 = jnp.min(jnp.where(rest == m2, lane, N_EXPERTS), axis=-1, keepdims=True)
    e2 = jnp.exp(m2 - m1)
    w1 = 1.0 / (1.0 + e2)
    return jnp.where(first, w1, jnp.where(lane == i2, e2 * w1, 0.0))


def _router_logits(xn_f32, wr_hi, wr_lo):
    x_hi = xn_f32.astype(BF16)
    x_lo = (xn_f32 - x_hi.astype(F32)).astype(BF16)
    both = jnp.dot(x_hi, jnp.concatenate([wr_hi, wr_lo], axis=1), preferred_element_type=F32)
    return (both[:, 0:N_EXPERTS] + both[:, N_EXPERTS:]
            + jnp.dot(x_lo, wr_hi, preferred_element_type=F32))


def _router_kernel(ylru_ref, ysb_ref, ysc_ref, wout_ref, h_ref, g_ref, wrh_ref, wrl_ref, ltri_ref,
                   hmid_ref, xn_ref, gate_ref, rank_ref, cum_ref, base_ref, ycat_ref, *,
                   subs_per_tile):
    @pl.when(pl.program_id(0) % subs_per_tile == 0)
    def _():
        base_ref[...] = jnp.zeros_like(base_ref)

    h = _mix_residual(ylru_ref, ysb_ref, ysc_ref, wout_ref, h_ref, ycat_ref)
    hmid_ref[...] = h
    xn = _rms_norm_rows(h, g_ref[...])
    xn_ref[...] = xn.astype(BF16)
    gates = _top2_gates(_router_logits(xn, wrh_ref[...], wrl_ref[...]))
    gate_ref[...] = gates
    sel = gates > 0.0
    picked = jnp.where(sel, 1.0, 0.0).astype(BF16)
    base = base_ref[...]
    for s in range(picked.shape[0] // MOE_SUB):
        rows = slice(s * MOE_SUB, (s + 1) * MOE_SUB)
        count = base + jnp.dot(ltri_ref[...], picked[rows, :], preferred_element_type=F32)
        rank_ref[rows, :] = jnp.where(sel[rows, :], count - 1.0, -1.0).astype(jnp.int32)
        base = count[MOE_SUB - 1:MOE_SUB, :]
        cum_ref[s] = base.astype(jnp.int32)
    base_ref[...] = base


def _outproj_router(ys, w_out, mix_layer, h, g, wr, moe_tile):
    t = h.shape[0]
    wr_hi = wr.astype(BF16)
    wr_lo = (wr - wr_hi.astype(F32)).astype(BF16)
    rt = min(ROUTER_TILE, moe_tile)
    assert moe_tile % rt == 0 and rt % MOE_SUB == 0
    r = lax.broadcasted_iota(jnp.int32, (MOE_SUB, MOE_SUB), 0)
    c = lax.broadcasted_iota(jnp.int32, (MOE_SUB, MOE_SUB), 1)
    ltri = jnp.where(c <= r, 1.0, 0.0).astype(BF16)
    nsteps = t // rt
    subs = rt // MOE_SUB
    full = lambda a: pl.BlockSpec(a.shape, lambda i: (0, 0))
    row = lambda cdim: pl.BlockSpec((rt, cdim), lambda i: (i, 0))
    return pl.pallas_call(
        functools.partial(_router_kernel, subs_per_tile=moe_tile // rt),
        grid=(nsteps,),
        in_specs=_mix_specs(w_out, mix_layer, rt) + [full(g), full(wr_hi), full(wr_lo), full(ltri)],
        out_specs=[row(D_MODEL), row(D_MODEL), row(N_EXPERTS), row(N_EXPERTS),
                   pl.BlockSpec((subs, 1, N_EXPERTS), lambda i: (i, 0, 0))],
        out_shape=[jax.ShapeDtypeStruct((t, D_MODEL), F32),
                   jax.ShapeDtypeStruct((t, D_MODEL), BF16),
                   jax.ShapeDtypeStruct((t, N_EXPERTS), F32),
                   jax.ShapeDtypeStruct((t, N_EXPERTS), jnp.int32),
                   jax.ShapeDtypeStruct((t // MOE_SUB, 1, N_EXPERTS), jnp.int32)],
        scratch_shapes=[pltpu.VMEM((1, N_EXPERTS), F32), pltpu.VMEM((rt, D_MIX), BF16)],
        compiler_params=_params("arbitrary"),
        name="outproj_norm_router",
    )(*ys, w_out, h, g, wr_hi, wr_lo, ltri)


def _moe_kernel(cum_ref, x_ref, rank_ref, gate_ref, wg_ref, wu_ref, wd_ref, o_ref,
                xc_ref, yc_ref, tmp_ref):
    i, e, f = pl.program_id(0), pl.program_id(1), pl.program_id(2)
    tile = x_ref.shape[0]
    nsub = tile // MOE_SUB
    base = (i * N_EXPERTS + e) * (nsub + 1)
    nchunk = (cum_ref[base + nsub] + MOE_CHUNK - 1) // MOE_CHUNK

    @pl.when((e == 0) & (f == 0))
    def _():
        o_ref[...] = jnp.zeros_like(o_ref)

    def overlaps(c, s):
        return (cum_ref[base + s] < (c + 1) * MOE_CHUNK) & (cum_ref[base + s + 1] > c * MOE_CHUNK)

    def match(c, s):
        rows = c * MOE_CHUNK + lax.broadcasted_iota(jnp.int32, (MOE_CHUNK, MOE_SUB), 0)
        return rank_ref[:, s * MOE_SUB:(s + 1) * MOE_SUB] == rows

    @pl.when(f == 0)
    def _():
        def gather(c, _):
            tmp_ref[...] = jnp.zeros_like(tmp_ref)
            for s in range(nsub):
                @pl.when(overlaps(c, s))
                def _():
                    p = jnp.where(match(c, s), 1.0, 0.0).astype(BF16)
                    tmp_ref[...] += jnp.dot(p, x_ref[s * MOE_SUB:(s + 1) * MOE_SUB, :],
                                            preferred_element_type=F32)
            xc_ref[pl.ds(pl.multiple_of(c * MOE_CHUNK, MOE_CHUNK), MOE_CHUNK), :] = (
                tmp_ref[...].astype(BF16))
            return 0
        lax.fori_loop(0, nchunk, gather, 0)

    def ffn_rows(start, m):
        rows = pl.ds(start, m)
        y = _swiglu_partial(xc_ref[rows, :], wg_ref, wu_ref, wd_ref, MOE_FF_SUB)
        tail = -m % MOE_CHUNK

        @pl.when(f == 0)
        def _():
            yc_ref[rows, :] = y
            if tail:
                yc_ref[pl.ds(start + m, tail), :] = jnp.zeros((tail, D_MODEL), F32)

        @pl.when(f != 0)
        def _():
            yc_ref[rows, :] += y

    count = cum_ref[base + nsub]
    lo = 0
    for m in [v for v in MOE_GROUP_ROWS if v <= xc_ref.shape[0]]:
        @pl.when((count > lo) & (count <= m))
        def _():
            ffn_rows(0, m)
        lo = m

    @pl.when(count > lo)
    def _():
        ffn_rows(0, lo)

        def extra(c, _):
            ffn_rows(pl.multiple_of(c * MOE_CHUNK, MOE_CHUNK), MOE_CHUNK)
            return 0
        lax.fori_loop(lo // MOE_CHUNK, nchunk, extra, 0)

    @pl.when(f == pl.num_programs(2) - 1)
    def _():
        def scatter(c, _):
            y = yc_ref[pl.ds(pl.multiple_of(c * MOE_CHUNK, MOE_CHUNK), MOE_CHUNK), :].astype(BF16)
            for s in range(nsub):
                @pl.when(overlaps(c, s))
                def _():
                    gate = gate_ref[:, s * MOE_SUB:(s + 1) * MOE_SUB]
                    pg = jnp.where(match(c, s), gate, 0.0).astype(BF16)
                    upd = lax.dot_general(pg, y, (((0,), (0,)), ((), ())),
                                          preferred_element_type=F32)
                    rows = slice(s * MOE_SUB, (s + 1) * MOE_SUB)
                    o_ref[rows, :] = (o_ref[rows, :].astype(F32) + upd).astype(BF16)
            return 0
        lax.fori_loop(0, nchunk, scatter, 0)


def _moe(xn, gates, rank, cum_end, wg, wu, wd, layer, moe_tile):
    t = xn.shape[0]
    nf = D_FF // FF_TILE
    ntile = t // moe_tile
    nsub = moe_tile // MOE_SUB
    capacity = -(-moe_tile // MOE_CHUNK) * MOE_CHUNK
    to_rows = lambda a: a.reshape(ntile, moe_tile, N_EXPERTS).transpose(0, 2, 1).reshape(
        ntile, N_EXPERTS, 1, moe_tile)
    cum = cum_end.reshape(ntile, nsub, N_EXPERTS).transpose(0, 2, 1)
    cum = jnp.concatenate([jnp.zeros((ntile, N_EXPERTS, 1), jnp.int32), cum], axis=-1).reshape(-1)
    meta = pl.BlockSpec((None, None, 1, moe_tile), lambda i, e, f, cum: (i, e, 0, 0))
    grid_spec = pltpu.PrefetchScalarGridSpec(
        num_scalar_prefetch=1,
        grid=(ntile, N_EXPERTS, nf),
        in_specs=[pl.BlockSpec((moe_tile, D_MODEL), lambda i, e, f, cum: (i, 0)),
                  meta, meta,
                  pl.BlockSpec((None, None, D_MODEL, FF_TILE),
                               lambda i, e, f, cum: (layer, e, 0, f)),
                  pl.BlockSpec((None, None, D_MODEL, FF_TILE),
                               lambda i, e, f, cum: (layer, e, 0, f)),
                  pl.BlockSpec((None, None, FF_TILE, D_MODEL),
                               lambda i, e, f, cum: (layer, e, f, 0))],
        out_specs=pl.BlockSpec((moe_tile, D_MODEL), lambda i, e, f, cum: (i, 0)),
        scratch_shapes=[pltpu.VMEM((capacity, D_MODEL), BF16),
                        pltpu.VMEM((capacity, D_MODEL), F32),
                        pltpu.VMEM((MOE_CHUNK, D_MODEL), F32)])
    return pl.pallas_call(
        _moe_kernel,
        grid_spec=grid_spec,
        out_shape=jax.ShapeDtypeStruct((t, D_MODEL), BF16),
        compiler_params=_params("parallel", "arbitrary", "arbitrary"),
        name="moe_swiglu",
    )(cum, xn, to_rows(rank), to_rows(gates), wg, wu, wd)


def _final_norm_kernel(h_ref, y_ref, g_ref, o_ref):
    o_ref[...] = _rms_norm_rows(h_ref[...] + y_ref[...].astype(F32), g_ref[...])


def _final_norm(h, y, g):
    t = h.shape[0]
    row = pl.BlockSpec((ROW_TILE, D_MODEL), lambda i: (i, 0))
    return pl.pallas_call(
        _final_norm_kernel,
        grid=(t // ROW_TILE,),
        in_specs=[row, row, pl.BlockSpec(g.shape, lambda i: (0, 0))],
        out_specs=row,
        out_shape=jax.ShapeDtypeStruct((t, D_MODEL), F32),
        compiler_params=_params("parallel"),
        name="final_norm",
    )(h, y, g)


def _block_diag(w):
    n = w.shape[0]
    eye = jnp.eye(n, dtype=w.dtype)
    return jnp.einsum("hij,hg->higj", w, eye).reshape(n * HEAD_DIM, n * HEAD_DIM)


def kernel(x, mix_norm_g, w_in, lru_conv_w, lru_conv_b, lru_wa, lru_ba, lru_wx, lru_bx, lru_lam,
           sc_conv_w, mix_out_g, w_out, ffn_norm_g, dense_wg, dense_wu, dense_wd,
           router_w, moe_wg, moe_wu, moe_wd, final_norm_g):
    batch, seq, _ = x.shape
    depth = w_in.shape[0]
    assert seq % (ATT_BLOCK * ATT_QBLOCKS) == 0 and seq % LRU_CHUNK == 0
    assert (batch * seq) % ROW_TILE == 0 and (batch * seq) % INPROJ_TILE == 0
    h = x.reshape(batch * seq, D_MODEL)
    row = lambda v: v.reshape(1, -1)
    moe_tile = min(MOE_TILE, batch * seq)
    assert (batch * seq) % moe_tile == 0 and moe_tile % MOE_SUB == 0
    w_in, w_out, dense_wg, dense_wu, dense_wd, moe_wg, moe_wu, moe_wd = (
        w.astype(BF16) for w in (w_in, w_out, dense_wg, dense_wu, dense_wd, moe_wg, moe_wu, moe_wd))
    pending = None
    for l in range(depth):
        h, lru, qkv, sc = _inproj(h, pending, row(mix_norm_g[l]), w_in, l)
        wgate = jnp.concatenate([_block_diag(lru_wa[l]), _block_diag(lru_wx[l])], axis=1).astype(BF16)
        bgate = jnp.concatenate([lru_ba[l], lru_bx[l]]).reshape(1, -1)
        g_mix = row(mix_out_g[l])
        y_lru, y_sc = _lru_sc(lru, sc, lru_conv_w[l], row(lru_conv_b[l]), wgate, bgate,
                              row(lru_lam[l]), sc_conv_w[l], g_mix[:, 0:D_LRU],
                              g_mix[:, D_LRU + D_SB:], batch, seq)
        y_sb = _sb_attention(qkv, g_mix[:, D_LRU:D_LRU + D_SB], batch, seq)
        ys = (y_lru, y_sb, y_sc)
        j = l // 2
        if l % 2 == 0:
            h = _outproj_ffn(ys, w_out, l, h, row(ffn_norm_g[l]), dense_wg, dense_wu, dense_wd, j)
            pending = None
        else:
            h, xn, gates, rank, cum_end = _outproj_router(ys, w_out, l, h, row(ffn_norm_g[l]),
                                                          router_w[j], moe_tile)
            pending = _moe(xn, gates, rank, cum_end, moe_wg, moe_wu, moe_wd, j, moe_tile)
    if pending is None:
        pending = jnp.zeros(h.shape, BF16)
    return _final_norm(h, pending, row(final_norm_g)).reshape(batch, seq, D_MODEL)
```

```python
import functools

import jax
import jax.numpy as jnp
from jax import lax
from jax.experimental import pallas as pl
from jax.experimental.pallas import tpu as pltpu

F32 = jnp.float32
BF16 = jnp.bfloat16

D_MODEL = 1024
HEAD_DIM = 64
D_LRU = 384
D_SB = 384
D_SC = 256
D_MIX = D_LRU + D_SB + D_SC
LRU_CONV = 4
SC_CONV = 3
LRU_C = 8.0
D_FF = 2816
N_EXPERTS = 8
EPS = 1e-6
F32_EXP_ZERO = -104.0

V7X_VMEM_LIMIT_BYTES = 56 * 1024 * 1024
SUBLANES = 8
LANES = 128

ROW_TILE = 512
INPROJ_TILE = 1024
LRU_CHUNK = 256
ATT_BLOCK = 256
ATT_QBLOCKS = 4
FF_TILE = 1408
MOE_TILE = 2048
MOE_SUB = 512
ROUTER_TILE = 1024
MOE_CHUNK = 256
MOE_GROUP_ROWS = (256, 512, 576, 640, 768)
FF_SUB = 256
MOE_FF_SUB = 704


def _params(*sem):
    return pltpu.CompilerParams(dimension_semantics=sem, vmem_limit_bytes=V7X_VMEM_LIMIT_BYTES)


def _rms_norm_rows(x, g):
    ms = jnp.mean(x * x, axis=-1, keepdims=True)
    return x * lax.rsqrt(ms + EPS) * g


def _sigmoid(x):
    return 1.0 / (1.0 + jnp.exp(-x))


def _head_norm(y, g, avg):
    ms = jnp.dot((y * y).astype(BF16), avg, preferred_element_type=F32)
    return (y * lax.rsqrt(ms + EPS) * g).astype(BF16)


def _head_avg_matrix(width):
    r = lax.broadcasted_iota(jnp.int32, (width, width), 0) // HEAD_DIM
    c = lax.broadcasted_iota(jnp.int32, (width, width), 1) // HEAD_DIM
    return jnp.where(r == c, 1.0 / HEAD_DIM, 0.0).astype(BF16)


def _inproj_kernel(*refs, has_residual):
    if has_residual:
        h_ref, y_ref, g_ref, w_ref, hout_ref, lru_ref, qkv_ref, sc_ref = refs
        h = h_ref[...] + y_ref[...].astype(F32)
        hout_ref[...] = h
    else:
        h_ref, g_ref, w_ref, lru_ref, qkv_ref, sc_ref = refs
        h = h_ref[...]
    xn = _rms_norm_rows(h, g_ref[...]).astype(BF16)
    c0, c1 = 2 * D_LRU, 2 * D_LRU + 3 * D_SB
    lru_ref[...] = jnp.dot(xn, w_ref[:, 0:c0], preferred_element_type=F32).astype(BF16)
    qkv_ref[...] = jnp.dot(xn, w_ref[:, c0:c1], preferred_element_type=F32).astype(BF16)
    sc_ref[...] = jnp.dot(xn, w_ref[:, c1:], preferred_element_type=F32).astype(BF16)


def _inproj(h, y, g, w_all, layer):
    t = h.shape[0]
    d_in = w_all.shape[2]
    row = lambda c: pl.BlockSpec((INPROJ_TILE, c), lambda i: (i, 0))
    acts = [h] if y is None else [h, y]
    proj_shapes = [jax.ShapeDtypeStruct((t, 2 * D_LRU), BF16),
                   jax.ShapeDtypeStruct((t, 3 * D_SB), BF16),
                   jax.ShapeDtypeStruct((t, d_in - 2 * D_LRU - 3 * D_SB), BF16)]
    proj_specs = [row(2 * D_LRU), row(3 * D_SB), row(3 * D_SC)]
    out = pl.pallas_call(
        functools.partial(_inproj_kernel, has_residual=y is not None),
        grid=(t // INPROJ_TILE,),
        in_specs=[row(D_MODEL)] * len(acts) + [
            pl.BlockSpec(g.shape, lambda i: (0, 0)),
            pl.BlockSpec((None, D_MODEL, d_in), lambda i: (layer, 0, 0))],
        out_specs=([] if y is None else [row(D_MODEL)]) + proj_specs,
        out_shape=([] if y is None else [jax.ShapeDtypeStruct((t, D_MODEL), F32)]) + proj_shapes,
        compiler_params=_params("parallel"),
        name="norm_inproj",
    )(*acts, g, w_all)
    return (h, *out) if y is None else tuple(out)


def _lru_sc_kernel(lru_ref, sc_ref, *refs):
    params, (ylru_ref, ysc_ref), scratch = refs[:10], refs[10:12], refs[12:]
    xhalo, phalo, abuf, bbuf, hcar = scratch
    pad = lru_ref.shape[1] // 2

    @pl.when(pl.program_id(0) == 0)
    def _():
        xhalo[...] = jnp.zeros_like(xhalo)
        phalo[...] = jnp.zeros_like(phalo)
        hcar[...] = jnp.zeros_like(hcar)
        abuf[:, 0:pad, :] = jnp.ones((abuf.shape[0], pad, D_LRU), F32)
        bbuf[:, 0:pad, :] = jnp.zeros((bbuf.shape[0], pad, D_LRU), F32)

    for r in range(lru_ref.shape[0]):
        _lru_sc_chain(lru_ref.at[r], sc_ref.at[r], *params, ylru_ref.at[r], ysc_ref.at[r],
                      *(buf.at[r] for buf in scratch))


def _lru_sc_chain(lru_ref, sc_ref, cw_ref, cb_ref, wgate_ref, bgate_ref, lam_ref, scw_ref,
                  glru_ref, gsc_ref, avg_ref, shift_ref,
                  ylru_ref, ysc_ref, xhalo, phalo, abuf, bbuf, hcar):
    tc = lru_ref.shape[0]
    pad = tc // 2

    def shifted(x_b, d):
        return jnp.dot(shift_ref[d - 1], x_b, preferred_element_type=F32)

    def with_halo(y, corr):
        return jnp.concatenate([y[0:SUBLANES, :] + corr, y[SUBLANES:, :]], axis=0)

    x_b = lru_ref[:, 0:D_LRU]
    conv = cb_ref[...] + cw_ref[LRU_CONV - 1:LRU_CONV, :] * x_b.astype(F32)
    corr = jnp.zeros((SUBLANES, D_LRU), F32)
    for d in range(1, LRU_CONV):
        w = cw_ref[LRU_CONV - 1 - d:LRU_CONV - d, :]
        conv = conv + w * shifted(x_b, d)
        corr = corr + w * xhalo[pl.ds(SUBLANES - d, SUBLANES), :]
    conv = with_halo(conv, corr)
    xhalo[0:SUBLANES, :] = x_b[tc - SUBLANES:tc, :].astype(F32)

    gates = jnp.dot(conv.astype(BF16), wgate_ref[...], preferred_element_type=F32) + bgate_ref[...]
    gate_r = _sigmoid(gates[:, 0:D_LRU])
    gate_i = _sigmoid(gates[:, D_LRU:])
    lam = lam_ref[...]
    log_sig_lam = jnp.minimum(lam, 0.0) - jnp.log1p(jnp.exp(-jnp.abs(lam)))
    log_a = (LRU_C * gate_r) * log_sig_lam
    a = jnp.exp(log_a)
    v = 1.0 - a * a
    u = jnp.where(v > 0.0, v * lax.rsqrt(v), 0.0) * (gate_i * conv)

    s = 1
    while s < tc:
        abuf[pad:pad + tc, :] = a
        bbuf[pad:pad + tc, :] = u
        a_sh = abuf[pl.ds(pad - s, tc), :]
        u_sh = bbuf[pl.ds(pad - s, tc), :]
        u = a * u_sh + u
        a = a * a_sh
        s *= 2
    h = a * hcar[0:1, :] + u
    hcar[...] = jnp.broadcast_to(h[tc - 1:tc, :], hcar.shape)

    ylru_ref[...] = _head_norm(h * jax.nn.gelu(lru_ref[:, D_LRU:].astype(F32)), glru_ref[...],
                               avg_ref[...])

    c_b = sc_ref[:, D_SC:2 * D_SC]
    s_b = sc_ref[:, 2 * D_SC:]
    p = c_b.astype(F32) * s_b.astype(F32)
    acc = scw_ref[SC_CONV - 1:SC_CONV, :] * p
    corr = jnp.zeros((SUBLANES, D_SC), F32)
    for d in range(1, SC_CONV):
        w = scw_ref[SC_CONV - 1 - d:SC_CONV - d, :]
        acc = acc + w * (shifted(c_b, d) * shifted(s_b, d))
        corr = corr + w * phalo[pl.ds(SUBLANES - d, SUBLANES), :]
    acc = with_halo(acc, corr)
    phalo[0:SUBLANES, :] = p[tc - SUBLANES:tc, :]
    ysc_ref[...] = _head_norm(sc_ref[:, 0:D_SC].astype(F32) * acc, gsc_ref[...],
                              avg_ref[0:D_SC, 0:D_SC])


def _lru_sc(lru, sc, cw, cb, wgate, bgate, lam, scw, g_lru, g_sc, batch, seq):
    tc = LRU_CHUNK
    nt = seq // tc
    avg = _head_avg_matrix(D_LRU)
    t_out = lax.broadcasted_iota(jnp.int32, (LRU_CONV - 1, tc, tc), 1)
    t_in = lax.broadcasted_iota(jnp.int32, (LRU_CONV - 1, tc, tc), 2)
    delay = lax.broadcasted_iota(jnp.int32, (LRU_CONV - 1, tc, tc), 0) + 1
    shift = jnp.where(t_in == t_out - delay, 1.0, 0.0).astype(BF16)
    row = lambda c: pl.BlockSpec((batch, tc, c), lambda t: (0, t, 0))
    full = lambda a: pl.BlockSpec(a.shape, lambda t: (0,) * a.ndim)
    per_row = lambda rows, c: pltpu.VMEM((batch, rows, c), F32)
    y_lru, y_sc = pl.pallas_call(
        _lru_sc_kernel,
        grid=(nt,),
        in_specs=[row(2 * D_LRU), row(3 * D_SC), full(cw), full(cb), full(wgate), full(bgate),
                  full(lam), full(scw), full(g_lru), full(g_sc), full(avg), full(shift)],
        out_specs=[row(D_LRU), row(D_SC)],
        out_shape=[jax.ShapeDtypeStruct((batch, seq, D_LRU), BF16),
                   jax.ShapeDtypeStruct((batch, seq, D_SC), BF16)],
        scratch_shapes=[per_row(2 * SUBLANES, D_LRU), per_row(2 * SUBLANES, D_SC),
                        per_row(tc + tc // 2, D_LRU), per_row(tc + tc // 2, D_LRU),
                        per_row(SUBLANES, D_LRU)],
        compiler_params=_params("arbitrary"),
        name="lru_shortconv",
    )(lru.reshape(batch, seq, -1), sc.reshape(batch, seq, -1), cw, cb, wgate, bgate, lam, scw,
      g_lru, g_sc, avg, shift)
    return y_lru.reshape(batch * seq, D_LRU), y_sc.reshape(batch * seq, D_SC)


def _sb_attn_kernel(q_ref, k_ref, v_ref, cm_ref, g_ref, avg_ref, o_ref, o_acc, c_acc):
    blk = cm_ref.shape[0]
    nqb = q_ref.shape[0] // blk
    npair = q_ref.shape[1] // LANES
    first = pl.program_id(1) * nqb
    rows = lambda b: slice(b * blk, (b + 1) * blk)
    head0 = lax.broadcasted_iota(jnp.int32, (1, LANES), 1) < HEAD_DIM
    lanes = lambda p: slice(p * LANES, (p + 1) * LANES)

    def stacked_q(b, p):
        q = q_ref[rows(b), lanes(p)] * jnp.asarray(HEAD_DIM ** -0.5, BF16)
        zero = jnp.zeros_like(q)
        return jnp.concatenate([jnp.where(head0, q, zero), jnp.where(head0, zero, q)], axis=0)

    def step(p, qq, j, o, carry, strict_mask=None, valid=None):
        start = pl.multiple_of(j * blk, blk)
        kt = k_ref[pl.ds(start, blk), lanes(p)]
        vt = v_ref[pl.ds(start, blk), lanes(p)]
        z = lax.dot_general(qq, kt, (((1,), (1,)), ((), ())), preferred_element_type=F32)
        soft = jnp.log(1.0 + jnp.exp(-jnp.abs(z)))
        log_beta = jnp.minimum(z, 0.0) - soft
        log_keep = log_beta - z
        keep = strict_mask if valid is None else valid
        if keep is not None:
            log_keep = jnp.where(keep, log_keep, 0.0)
        cs = jnp.dot(log_keep.astype(BF16), cm_ref[...], preferred_element_type=F32)
        later = cs[:, 0:blk] + jnp.concatenate([carry] * (blk // LANES), axis=1)
        w = jnp.exp(log_beta + later)
        if keep is not None:
            w = jnp.where(keep, w, 0.0)
        o = o + jnp.dot(w.astype(BF16), vt, preferred_element_type=F32)
        return o, carry + cs[:, blk:]

    row = lax.broadcasted_iota(jnp.int32, (2 * blk, blk), 0)
    col = lax.broadcasted_iota(jnp.int32, (2 * blk, blk), 1)
    strict = col < jnp.where(row >= blk, row - blk, row)
    zeros = jnp.zeros((2 * blk, LANES), F32)
    qqs = [[stacked_q(b, p) for p in range(npair)] for b in range(nqb)]
    lives = []
    for b in range(nqb):
        i = first + b
        live = None
        for p in range(npair):
            o, carry = step(p, qqs[b][p], i, zeros, zeros, strict_mask=strict)
            o, carry = step(p, qqs[b][p], jnp.maximum(i - 1, 0), o, carry, valid=i > 0)
            o_acc[b * npair + p] = o
            c_acc[b * npair + p] = carry
            top = jnp.max(carry)
            live = top if live is None else jnp.maximum(live, top)
        lives.append(live)

    for b in range(nqb):
        i = first + b

        def cond(state, i=i):
            n, live = state
            return jnp.logical_and(n < i, live > F32_EXP_ZERO)

        def body(state, i=i, b=b):
            n, _ = state
            live = None
            for p in range(npair):
                c = b * npair + p
                o, carry = step(p, qqs[b][p], i - 1 - n, o_acc[c], c_acc[c])
                o_acc[c] = o
                c_acc[c] = carry
                top = jnp.max(carry)
                live = top if live is None else jnp.maximum(live, top)
            return n + 1, live

        lax.while_loop(cond, body, (jnp.int32(1), lives[b]))

    for b in range(nqb):
        for p in range(npair):
            o = o_acc[b * npair + p]
            o_ref[rows(b), lanes(p)] = _head_norm(jnp.where(head0, o[0:blk, :], o[blk:, :]),
                                                  g_ref[:, lanes(p)], avg_ref[...])


def _cumsum_matrix(blk):
    j = lax.broadcasted_iota(jnp.int32, (blk, blk + LANES), 0)
    s = lax.broadcasted_iota(jnp.int32, (blk, blk + LANES), 1)
    return jnp.where((j > s) | (s >= blk), 1.0, 0.0).astype(BF16)


def _sb_attention(qkv, g_sb, batch, seq):
    blk = ATT_BLOCK
    rows = ATT_QBLOCKS * blk
    nq = seq // rows
    npair = D_SB // LANES
    cm = _cumsum_matrix(blk)
    avg = _head_avg_matrix(LANES)
    return pl.pallas_call(
        _sb_attn_kernel,
        grid=(batch, nq),
        in_specs=[pl.BlockSpec((rows, D_SB), lambda b, i: (b * nq + i, 0)),
                  pl.BlockSpec((seq, D_SB), lambda b, i: (b, 1)),
                  pl.BlockSpec((seq, D_SB), lambda b, i: (b, 2)),
                  pl.BlockSpec(cm.shape, lambda b, i: (0, 0)),
                  pl.BlockSpec(g_sb.shape, lambda b, i: (0, 0)),
                  pl.BlockSpec(avg.shape, lambda b, i: (0, 0))],
        out_specs=pl.BlockSpec((rows, D_SB), lambda b, i: (b * nq + i, 0)),
        out_shape=jax.ShapeDtypeStruct((batch * seq, D_SB), BF16),
        scratch_shapes=[pltpu.VMEM((ATT_QBLOCKS * npair, 2 * blk, LANES), F32),
                        pltpu.VMEM((ATT_QBLOCKS * npair, 2 * blk, LANES), F32)],
        compiler_params=_params("parallel", "arbitrary"),
        name="stickbreak_attn",
    )(qkv, qkv, qkv, cm, g_sb, avg)


def _mix_residual(ylru_ref, ysb_ref, ysc_ref, w_ref, h_ref, ycat_ref):
    lo = 0
    for y_ref in (ylru_ref, ysb_ref, ysc_ref):
        c = y_ref.shape[1]
        ycat_ref[:, lo:lo + c] = y_ref[...]
        lo += c
    return h_ref[...] + jnp.dot(ycat_ref[...], w_ref[...], preferred_element_type=F32)


def _mix_specs(w_out_all, layer, rows):
    row = lambda cdim: pl.BlockSpec((rows, cdim), lambda i: (i, 0))
    return [row(D_LRU), row(D_SB), row(D_SC),
            pl.BlockSpec((None,) + w_out_all.shape[1:], lambda i: (layer, 0, 0)), row(D_MODEL)]


def _swiglu_partial(xn, wg_ref, wu_ref, wd_ref, sub):
    width = wg_ref.shape[1]
    out = None
    for lo in range(0, width, sub):
        hi = min(lo + sub, width)
        a = jnp.dot(xn, wg_ref[:, lo:hi], preferred_element_type=F32)
        b = jnp.dot(xn, wu_ref[:, lo:hi], preferred_element_type=F32)
        hid = (a * _sigmoid(a) * b).astype(BF16)
        part = jnp.dot(hid, wd_ref[lo:hi, :], preferred_element_type=F32)
        out = part if out is None else out + part
    return out


def _ffn_kernel(ylru_ref, ysb_ref, ysc_ref, wout_ref, h_ref, g_ref, wg_ref, wu_ref, wd_ref, o_ref,
                ycat_ref):
    h = _mix_residual(ylru_ref, ysb_ref, ysc_ref, wout_ref, h_ref, ycat_ref)
    xn = _rms_norm_rows(h, g_ref[...]).astype(BF16)
    o_ref[...] = h + _swiglu_partial(xn, wg_ref, wu_ref, wd_ref, FF_SUB)


def _outproj_ffn(ys, w_out, mix_layer, h, g, wg, wu, wd, layer):
    t = h.shape[0]
    full = lambda a: pl.BlockSpec(a.shape, lambda i: (0, 0))
    stacked = lambda a: pl.BlockSpec((None,) + a.shape[1:], lambda i: (layer, 0, 0))
    return pl.pallas_call(
        _ffn_kernel,
        grid=(t // ROW_TILE,),
        in_specs=_mix_specs(w_out, mix_layer, ROW_TILE) + [full(g), stacked(wg), stacked(wu),
                                                            stacked(wd)],
        out_specs=pl.BlockSpec((ROW_TILE, D_MODEL), lambda i: (i, 0)),
        out_shape=jax.ShapeDtypeStruct((t, D_MODEL), F32),
        scratch_shapes=[pltpu.VMEM((ROW_TILE, D_MIX), BF16)],
        compiler_params=_params("parallel"),
        name="outproj_norm_swiglu",
    )(*ys, w_out, h, g, wg, wu, wd)


def _top2_gates(logits):
    lane = lax.broadcasted_iota(jnp.int32, logits.shape, 1)
    m1 = jnp.max(logits, axis=-1, keepdims=True)
    i1 = jnp.min(jnp.where(logits == m1, lane, N_EXPERTS), axis=-1, keepdims=True)
    first = lane == i1
    rest = jnp.where(first, -jnp.inf, logits)
    m2 = jnp.max(rest, axis=-1, keepdims=True)
    i2 = jnp.min(jnp.where(rest == m2, lane, N_EXPERTS), axis=-1, keepdims=True)
    e2 = jnp.exp(m2 - m1)
    w1 = 1.0 / (1.0 + e2)
    return jnp.where(first, w1, jnp.where(lane == i2, e2 * w1, 0.0))


def _router_logits(xn_f32, wr_hi, wr_lo):
    x_hi = xn_f32.astype(BF16)
    x_lo = (xn_f32 - x_hi.astype(F32)).astype(BF16)
    both = jnp.dot(x_hi, jnp.concatenate([wr_hi, wr_lo], axis=1), preferred_element_type=F32)
    return (both[:, 0:N_EXPERTS] + both[:, N_EXPERTS:]
            + jnp.dot(x_lo, wr_hi, preferred_element_type=F32))


def _router_kernel(ylru_ref, ysb_ref, ysc_ref, wout_ref, h_ref, g_ref, wrh_ref, wrl_ref, ltri_ref,
                   hmid_ref, xn_ref, gate_ref, rank_ref, cum_ref, base_ref, ycat_ref, *,
                   subs_per_tile):
    @pl.when(pl.program_id(0) % subs_per_tile == 0)
    def _():
        base_ref[...] = jnp.zeros_like(base_ref)

    h = _mix_residual(ylru_ref, ysb_ref, ysc_ref, wout_ref, h_ref, ycat_ref)
    hmid_ref[...] = h
    xn = _rms_norm_rows(h, g_ref[...])
    xn_ref[...] = xn.astype(BF16)
    gates = _top2_gates(_router_logits(xn, wrh_ref[...], wrl_ref[...]))
    gate_ref[...] = gates
    sel = gates > 0.0
    picked = jnp.where(sel, 1.0, 0.0).astype(BF16)
    base = base_ref[...]
    for s in range(picked.shape[0] // MOE_SUB):
        rows = slice(s * MOE_SUB, (s + 1) * MOE_SUB)
        count = base + jnp.dot(ltri_ref[...], picked[rows, :], preferred_element_type=F32)
        rank_ref[rows, :] = jnp.where(sel[rows, :], count - 1.0, -1.0).astype(jnp.int32)
        base = count[MOE_SUB - 1:MOE_SUB, :]
        cum_ref[s] = base.astype(jnp.int32)
    base_ref[...] = base


def _outproj_router(ys, w_out, mix_layer, h, g, wr, moe_tile):
    t = h.shape[0]
    wr_hi = wr.astype(BF16)
    wr_lo = (wr - wr_hi.astype(F32)).astype(BF16)
    rt = min(ROUTER_TILE, moe_tile)
    assert moe_tile % rt == 0 and rt % MOE_SUB == 0
    r = lax.broadcasted_iota(jnp.int32, (MOE_SUB, MOE_SUB), 0)
    c = lax.broadcasted_iota(jnp.int32, (MOE_SUB, MOE_SUB), 1)
    ltri = jnp.where(c <= r, 1.0, 0.0).astype(BF16)
    nsteps = t // rt
    subs = rt // MOE_SUB
    full = lambda a: pl.BlockSpec(a.shape, lambda i: (0, 0))
    row = lambda cdim: pl.BlockSpec((rt, cdim), lambda i: (i, 0))
    return pl.pallas_call(
        functools.partial(_router_kernel, subs_per_tile=moe_tile // rt),
        grid=(nsteps,),
        in_specs=_mix_specs(w_out, mix_layer, rt) + [full(g), full(wr_hi), full(wr_lo), full(ltri)],
        out_specs=[row(D_MODEL), row(D_MODEL), row(N_EXPERTS), row(N_EXPERTS),
                   pl.BlockSpec((subs, 1, N_EXPERTS), lambda i: (i, 0, 0))],
        out_shape=[jax.ShapeDtypeStruct((t, D_MODEL), F32),
                   jax.ShapeDtypeStruct((t, D_MODEL), BF16),
                   jax.ShapeDtypeStruct((t, N_EXPERTS), F32),
                   jax.ShapeDtypeStruct((t, N_EXPERTS), jnp.int32),
                   jax.ShapeDtypeStruct((t // MOE_SUB, 1, N_EXPERTS), jnp.int32)],
        scratch_shapes=[pltpu.VMEM((1, N_EXPERTS), F32), pltpu.VMEM((rt, D_MIX), BF16)],
        compiler_params=_params("arbitrary"),
        name="outproj_norm_router",
    )(*ys, w_out, h, g, wr_hi, wr_lo, ltri)


def _moe_kernel(cum_ref, x_ref, rank_ref, gate_ref, wg_ref, wu_ref, wd_ref, o_ref,
                xc_ref, yc_ref, tmp_ref):
    i, e, f = pl.program_id(0), pl.program_id(1), pl.program_id(2)
    tile = x_ref.shape[0]
    nsub = tile // MOE_SUB
    base = (i * N_EXPERTS + e) * (nsub + 1)
    nchunk = (cum_ref[base + nsub] + MOE_CHUNK - 1) // MOE_CHUNK

    @pl.when((e == 0) & (f == 0))
    def _():
        o_ref[...] = jnp.zeros_like(o_ref)

    def overlaps(c, s):
        return (cum_ref[base + s] < (c + 1) * MOE_CHUNK) & (cum_ref[base + s + 1] > c * MOE_CHUNK)

    def match(c, s):
        rows = c * MOE_CHUNK + lax.broadcasted_iota(jnp.int32, (MOE_CHUNK, MOE_SUB), 0)
        return rank_ref[:, s * MOE_SUB:(s + 1) * MOE_SUB] == rows

    @pl.when(f == 0)
    def _():
        def gather(c, _):
            tmp_ref[...] = jnp.zeros_like(tmp_ref)
            for s in range(nsub):
                @pl.when(overlaps(c, s))
                def _():
                    p = jnp.where(match(c, s), 1.0, 0.0).astype(BF16)
                    tmp_ref[...] += jnp.dot(p, x_ref[s * MOE_SUB:(s + 1) * MOE_SUB, :],
                                            preferred_element_type=F32)
            xc_ref[pl.ds(pl.multiple_of(c * MOE_CHUNK, MOE_CHUNK), MOE_CHUNK), :] = (
                tmp_ref[...].astype(BF16))
            return 0
        lax.fori_loop(0, nchunk, gather, 0)

    def ffn_rows(start, m):
        rows = pl.ds(start, m)
        y = _swiglu_partial(xc_ref[rows, :], wg_ref, wu_ref, wd_ref, MOE_FF_SUB)
        tail = -m % MOE_CHUNK

        @pl.when(f == 0)
        def _():
            yc_ref[rows, :] = y
            if tail:
                yc_ref[pl.ds(start + m, tail), :] = jnp.zeros((tail, D_MODEL), F32)

        @pl.when(f != 0)
        def _():
            yc_ref[rows, :] += y

    count = cum_ref[base + nsub]
    lo = 0
    for m in [v for v in MOE_GROUP_ROWS if v <= xc_ref.shape[0]]:
        @pl.when((count > lo) & (count <= m))
        def _():
            ffn_rows(0, m)
        lo = m

    @pl.when(count > lo)
    def _():
        ffn_rows(0, lo)

        def extra(c, _):
            ffn_rows(pl.multiple_of(c * MOE_CHUNK, MOE_CHUNK), MOE_CHUNK)
            return 0
        lax.fori_loop(lo // MOE_CHUNK, nchunk, extra, 0)

    @pl.when(f == pl.num_programs(2) - 1)
    def _():
        def scatter(c, _):
            y = yc_ref[pl.ds(pl.multiple_of(c * MOE_CHUNK, MOE_CHUNK), MOE_CHUNK), :].astype(BF16)
            for s in range(nsub):
                gate = gate_ref[:, s * MOE_SUB:(s + 1) * MOE_SUB]
                pg = jnp.where(match(c, s), gate, 0.0).astype(BF16)
                upd = lax.dot_general(pg, y, (((0,), (0,)), ((), ())),
                                      preferred_element_type=F32)
                rows = slice(s * MOE_SUB, (s + 1) * MOE_SUB)
                o_ref[rows, :] = (o_ref[rows, :].astype(F32) + upd).astype(BF16)
            return 0
        lax.fori_loop(0, nchunk, scatter, 0)


def _moe(xn, gates, rank, cum_end, wg, wu, wd, layer, moe_tile):
    t = xn.shape[0]
    nf = D_FF // FF_TILE
    ntile = t // moe_tile
    nsub = moe_tile // MOE_SUB
    capacity = -(-moe_tile // MOE_CHUNK) * MOE_CHUNK
    to_rows = lambda a: a.reshape(ntile, moe_tile, N_EXPERTS).transpose(0, 2, 1).reshape(
        ntile, N_EXPERTS, 1, moe_tile)
    cum = cum_end.reshape(ntile, nsub, N_EXPERTS).transpose(0, 2, 1)
    cum = jnp.concatenate([jnp.zeros((ntile, N_EXPERTS, 1), jnp.int32), cum], axis=-1).reshape(-1)
    meta = pl.BlockSpec((None, None, 1, moe_tile), lambda i, e, f, cum: (i, e, 0, 0))
    grid_spec = pltpu.PrefetchScalarGridSpec(
        num_scalar_prefetch=1,
        grid=(ntile, N_EXPERTS, nf),
        in_specs=[pl.BlockSpec((moe_tile, D_MODEL), lambda i, e, f, cum: (i, 0)),
                  meta, meta,
                  pl.BlockSpec((None, None, D_MODEL, FF_TILE),
                               lambda i, e, f, cum: (layer, e, 0, f)),
                  pl.BlockSpec((None, None, D_MODEL, FF_TILE),
                               lambda i, e, f, cum: (layer, e, 0, f)),
                  pl.BlockSpec((None, None, FF_TILE, D_MODEL),
                               lambda i, e, f, cum: (layer, e, f, 0))],
        out_specs=pl.BlockSpec((moe_tile, D_MODEL), lambda i, e, f, cum: (i, 0)),
        scratch_shapes=[pltpu.VMEM((capacity, D_MODEL), BF16),
                        pltpu.VMEM((capacity, D_MODEL), F32),
                        pltpu.VMEM((MOE_CHUNK, D_MODEL), F32)])
    return pl.pallas_call(
        _moe_kernel,
        grid_spec=grid_spec,
        out_shape=jax.ShapeDtypeStruct((t, D_MODEL), BF16),
        compiler_params=_params("parallel", "arbitrary", "arbitrary"),
        name="moe_swiglu",
    )(cum, xn, to_rows(rank), to_rows(gates), wg, wu, wd)


def _final_norm_kernel(h_ref, y_ref, g_ref, o_ref):
    o_ref[...] = _rms_norm_rows(h_ref[...] + y_ref[...].astype(F32), g_ref[...])


def _final_norm(h, y, g):
    t = h.shape[0]
    row = pl.BlockSpec((ROW_TILE, D_MODEL), lambda i: (i, 0))
    return pl.pallas_call(
        _final_norm_kernel,
        grid=(t // ROW_TILE,),
        in_specs=[row, row, pl.BlockSpec(g.shape, lambda i: (0, 0))],
        out_specs=row,
        out_shape=jax.ShapeDtypeStruct((t, D_MODEL), F32),
        compiler_params=_params("parallel"),
        name="final_norm",
    )(h, y, g)


def _block_diag(w):
    n = w.shape[0]
    eye = jnp.eye(n, dtype=w.dtype)
    return jnp.einsum("hij,hg->higj", w, eye).reshape(n * HEAD_DIM, n * HEAD_DIM)


def kernel(x, mix_norm_g, w_in, lru_conv_w, lru_conv_b, lru_wa, lru_ba, lru_wx, lru_bx, lru_lam,
           sc_conv_w, mix_out_g, w_out, ffn_norm_g, dense_wg, dense_wu, dense_wd,
           router_w, moe_wg, moe_wu, moe_wd, final_norm_g):
    batch, seq, _ = x.shape
    depth = w_in.shape[0]
    assert seq % (ATT_BLOCK * ATT_QBLOCKS) == 0 and seq % LRU_CHUNK == 0
    assert (batch * seq) % ROW_TILE == 0 and (batch * seq) % INPROJ_TILE == 0
    h = x.reshape(batch * seq, D_MODEL)
    row = lambda v: v.reshape(1, -1)
    moe_tile = min(MOE_TILE, batch * seq)
    assert (batch * seq) % moe_tile == 0 and moe_tile % MOE_SUB == 0
    w_in, w_out, dense_wg, dense_wu, dense_wd, moe_wg, moe_wu, moe_wd = (
        w.astype(BF16) for w in (w_in, w_out, dense_wg, dense_wu, dense_wd, moe_wg, moe_wu, moe_wd))
    pending = None
    for l in range(depth):
        h, lru, qkv, sc = _inproj(h, pending, row(mix_norm_g[l]), w_in, l)
        wgate = jnp.concatenate([_block_diag(lru_wa[l]), _block_diag(lru_wx[l])], axis=1).astype(BF16)
        bgate = jnp.concatenate([lru_ba[l], lru_bx[l]]).reshape(1, -1)
        g_mix = row(mix_out_g[l])
        y_lru, y_sc = _lru_sc(lru, sc, lru_conv_w[l], row(lru_conv_b[l]), wgate, bgate,
                              row(lru_lam[l]), sc_conv_w[l], g_mix[:, 0:D_LRU],
                              g_mix[:, D_LRU + D_SB:], batch, seq)
        y_sb = _sb_attention(qkv, g_mix[:, D_LRU:D_LRU + D_SB], batch, seq)
        ys = (y_lru, y_sb, y_sc)
        j = l // 2
        if l % 2 == 0:
            h = _outproj_ffn(ys, w_out, l, h, row(ffn_norm_g[l]), dense_wg, dense_wu, dense_wd, j)
            pending = None
        else:
            h, xn, gates, rank, cum_end = _outproj_router(ys, w_out, l, h, row(ffn_norm_g[l]),
                                                          router_w[j], moe_tile)
            pending = _moe(xn, gates, rank, cum_end, moe_wg, moe_wu, moe_wd, j, moe_tile)
    if pending is None:
        pending = jnp.zeros(h.shape, BF16)
    return _final_norm(h, pending, row(final_norm_g)).reshape(batch, seq, D_MODEL)
```
